```python
import math
import jax, jax.numpy as jnp
from jax import lax
import numpy as np

D_MODEL = 1024
BATCH = 16
SEQ = 2048
DEPTH = 2

D_CONV_A = D_MODEL // 2
SHORT_CONV_W = 3
RET_HEADS = 8
RET_HEAD_DIM = D_MODEL // RET_HEADS
D_RET = RET_HEADS * RET_HEAD_DIM
RET_CHUNK = 128
ROPE_BASE = 10000.0
D_CONF = D_MODEL // 2
CONF_KERNEL = 31
N_BRANCH = 3
D_IN = 3 * D_CONV_A + 4 * D_RET + 2 * D_CONF + N_BRANCH * D_MODEL
N_EXPERTS = 32
TOP_K = 4
D_EXPERT = D_MODEL
SWIGLU_LIMIT = 7.0
SWIGLU_ALPHA = 1.702
MOE_BLOCK = 128
EPS = 1e-6

kernel_name = 'conditioned_hybrid_conv_retention_conformer_moe'


def rms_norm(x, g):
    xf = x.astype(jnp.float32)
    y = xf * lax.rsqrt(jnp.mean(xf * xf, axis=-1, keepdims=True) + EPS)
    return (y * g.astype(jnp.float32)).astype(x.dtype)


def layer_norm(x, g=None, b=None):
    xf = x.astype(jnp.float32)
    mu = jnp.mean(xf, axis=-1, keepdims=True)
    var = jnp.mean(jnp.square(xf - mu), axis=-1, keepdims=True)
    y = (xf - mu) * lax.rsqrt(var + EPS)
    if g is not None:
        y = y * g.astype(jnp.float32) + b.astype(jnp.float32)
    return y.astype(x.dtype)


def modulate(h, shift, scale):
    return h * (1 + scale[:, None, :]) + shift[:, None, :]


def causal_dwconv(x, w):
    K, C = w.shape
    return lax.conv_general_dilated(
        x, w[:, None, :].astype(x.dtype), window_strides=(1,), padding=[(K - 1, 0)],
        dimension_numbers=('NWC', 'WIO', 'NWC'), feature_group_count=C)


def rope_tables(positions, dtype):
    inv_freq = 1.0 / (ROPE_BASE ** (jnp.arange(0, RET_HEAD_DIM, 2, dtype=jnp.float32) / RET_HEAD_DIM))
    ang = positions.astype(jnp.float32)[..., None] * inv_freq
    return jnp.cos(ang)[:, :, None, :].astype(dtype), jnp.sin(ang)[:, :, None, :].astype(dtype)


def apply_rope(t, cos, sin):
    t1, t2 = jnp.split(t, 2, axis=-1)
    return jnp.concatenate([t1 * cos - t2 * sin, t2 * cos + t1 * sin], axis=-1)


def retention(q, k, v):
    Bt, S, H, d = q.shape
    nC = S // RET_CHUNK
    dt = q.dtype
    q = q * (RET_HEAD_DIM ** -0.5)
    to_chunks = lambda t: t.reshape(Bt, nC, RET_CHUNK, H, d).transpose(1, 0, 3, 2, 4)
    qc, kc, vc = to_chunks(q), to_chunks(k), to_chunks(v)
    gamma = 1.0 - jnp.power(2.0, -5.0 - jnp.arange(H, dtype=jnp.float32))
    lg = jnp.log(gamma)
    idx = jnp.arange(RET_CHUNK, dtype=jnp.float32)
    rel = idx[:, None] - idx[None, :]
    dmask = jnp.where(rel[None] >= 0, jnp.exp(jnp.maximum(rel, 0.0)[None] * lg[:, None, None]), 0.0).astype(dt)
    xi = jnp.exp((idx + 1.0)[None] * lg[:, None]).astype(dt)
    zeta = jnp.exp((RET_CHUNK - 1.0 - idx)[None] * lg[:, None]).astype(dt)
    g_chunk = jnp.exp(RET_CHUNK * lg).astype(dt)

    def step(state, inp):
        qi, ki, vi = inp
        scores = jnp.einsum('bhid,bhjd->bhij', qi, ki) * dmask
        inner = jnp.einsum('bhij,bhje->bhie', scores, vi)
        cross = jnp.einsum('bhid,bhde->bhie', qi, state) * xi[None, :, :, None]
        state = state * g_chunk[None, :, None, None] + jnp.einsum('bhjd,bhje->bhde', ki * zeta[None, :, :, None], vi)
        return state, inner + cross

    state0 = jnp.zeros((Bt, H, d, d), dt)
    _, out = lax.scan(step, state0, (qc, kc, vc))
    return out.transpose(1, 0, 3, 2, 4).reshape(Bt, S, H, d)


def hybrid_mixer(h, cos, sin, w_in, conv_a_w, w_a_out, ret_norm_g, w_b_out, glu_b,
                 conv_c_w, conv_c_b, ln_c_g, ln_c_b, w_c_out, w_o):
    Bt, S, _ = h.shape
    proj = h @ w_in
    s1 = 3 * D_CONV_A
    s2 = s1 + 4 * D_RET
    s3 = s2 + 2 * D_CONF
    a_in, ret_in, conf_in, gate_in = jnp.split(proj, [s1, s2, s3], axis=-1)
    b_g, c_g, xa = jnp.split(a_in, 3, axis=-1)
    y_a = (b_g * causal_dwconv(c_g * xa, conv_a_w)) @ w_a_out
    q, k, v, og = jnp.split(ret_in, 4, axis=-1)
    heads = lambda t: t.reshape(Bt, S, RET_HEADS, RET_HEAD_DIM)
    r = retention(apply_rope(heads(q), cos, sin), apply_rope(heads(k), cos, sin), heads(v))
    r = layer_norm(r).reshape(Bt, S, D_RET) * ret_norm_g
    y_b = (jax.nn.silu(og) * r) @ w_b_out
    ga, gb = jnp.split(conf_in + glu_b, 2, axis=-1)
    u = causal_dwconv(ga * jax.nn.sigmoid(gb), conv_c_w) + conv_c_b
    y_c = jax.nn.silu(layer_norm(u, ln_c_g, ln_c_b)) @ w_c_out
    s_a, s_b, s_c = jnp.split(jax.nn.sigmoid(gate_in), 3, axis=-1)
    return (s_a * y_a + s_b * y_b + s_c * y_c) @ w_o


def moe(h, router_w, router_b, w1, b1, w2, b2):
    Bt, S, D = h.shape
    N = Bt * S
    A = N * TOP_K
    xt = h.reshape(N, D)
    logits = xt.astype(jnp.float32) @ router_w.astype(jnp.float32) + router_b.astype(jnp.float32)
    top_v, top_i = lax.top_k(logits, TOP_K)
    probs = jax.nn.softmax(top_v, axis=-1)
    e = top_i.reshape(A)
    tok = jnp.repeat(jnp.arange(N, dtype=jnp.int32), TOP_K)
    wts = probs.reshape(A)
    order = jnp.argsort(e)
    e_s, tok_s, w_s = e[order], tok[order], wts[order]
    counts = jnp.bincount(e, length=N_EXPERTS)
    starts = jnp.cumsum(counts) - counts
    padded = ((counts + MOE_BLOCK - 1) // MOE_BLOCK) * MOE_BLOCK
    pends = jnp.cumsum(padded)
    pstarts = pends - padded
    dest = pstarts[e_s] + jnp.arange(A, dtype=jnp.int32) - starts[e_s]
    n_blocks = (A + N_EXPERTS * (MOE_BLOCK - 1) + MOE_BLOCK - 1) // MOE_BLOCK
    rows = n_blocks * MOE_BLOCK
    xbuf = jnp.zeros((rows, D), h.dtype).at[dest].set(xt[tok_s])
    block_expert = jnp.minimum(
        jnp.searchsorted(pends, jnp.arange(n_blocks, dtype=pends.dtype) * MOE_BLOCK, side='right'),
        N_EXPERTS - 1)

    def expert(args):
        xb, ei = args
        gu = xb @ w1[ei] + b1[ei]
        glu, lin = jnp.split(gu, 2, axis=-1)
        glu = jnp.minimum(glu, SWIGLU_LIMIT)
        lin = jnp.clip(lin, -SWIGLU_LIMIT, SWIGLU_LIMIT)
        act = glu * jax.nn.sigmoid(SWIGLU_ALPHA * glu) * (lin + 1)
        return act @ w2[ei] + b2[ei]

    ybuf = lax.map(expert, (xbuf.reshape(n_blocks, MOE_BLOCK, D), block_expert)).reshape(rows, D)
    out = jnp.zeros((N, D), h.dtype).at[tok_s].add(ybuf[dest] * w_s[:, None].astype(h.dtype))
    return out.reshape(Bt, S, D)


def setup_inputs(seed: int = 0) -> dict:
    key = jax.random.key(seed)
    ks = jax.random.split(key, 32)
    f32 = jnp.float32
    nrm = lambda k, shape, scale: jax.random.normal(k, shape, f32) * scale
    D, L, E, F = D_MODEL, DEPTH, N_EXPERTS, D_EXPERT
    offset = jax.random.randint(ks[2], (BATCH, 1), 0, 4096)
    positions = (offset + jnp.arange(SEQ)[None, :]).astype(jnp.int32)
    return {
        'x': nrm(ks[0], (BATCH, SEQ, D), 1.0),
        'c': nrm(ks[1], (BATCH, D), 1.0),
        'positions': positions,
        'mod_w': nrm(ks[3], (L, D, 6 * D), 0.5 * D ** -0.5),
        'mod_b': nrm(ks[4], (L, 6 * D), 0.02),
        'norm1_g': 1.0 + nrm(ks[5], (L, D), 0.02),
        'w_in': nrm(ks[6], (L, D, D_IN), D ** -0.5),
        'conv_a_w': nrm(ks[7], (L, SHORT_CONV_W, D_CONV_A), SHORT_CONV_W ** -0.5),
        'w_a_out': nrm(ks[8], (L, D_CONV_A, D), D_CONV_A ** -0.5),
        'ret_norm_g': 1.0 + nrm(ks[9], (L, D_RET), 0.02),
        'w_b_out': nrm(ks[10], (L, D_RET, D), D_RET ** -0.5),
        'glu_b': nrm(ks[11], (L, 2 * D_CONF), 0.02),
        'conv_c_w': nrm(ks[12], (L, CONF_KERNEL, D_CONF), CONF_KERNEL ** -0.5),
        'conv_c_b': nrm(ks[13], (L, D_CONF), 0.02),
        'ln_c_g': 1.0 + nrm(ks[14], (L, D_CONF), 0.02),
        'ln_c_b': nrm(ks[15], (L, D_CONF), 0.02),
        'w_c_out': nrm(ks[16], (L, D_CONF, D), D_CONF ** -0.5),
        'w_o': nrm(ks[17], (L, D, D), D ** -0.5),
        'norm2_g': 1.0 + nrm(ks[18], (L, D), 0.02),
        'router_w': nrm(ks[19], (L, D, E), D ** -0.5),
        'router_b': nrm(ks[20], (L, E), 0.01),
        'exp_w1': nrm(ks[21], (L, E, D, 2 * F), D ** -0.5),
        'exp_b1': nrm(ks[22], (L, E, 2 * F), 0.02),
        'exp_w2': nrm(ks[23], (L, E, F, D), F ** -0.5),
        'exp_b2': nrm(ks[24], (L, E, D), 0.02),
        'final_g': 1.0 + nrm(ks[25], (D,), 0.02),
    }


def reference(x, c, positions, mod_w, mod_b, norm1_g, w_in, conv_a_w, w_a_out, ret_norm_g, w_b_out,
              glu_b, conv_c_w, conv_c_b, ln_c_g, ln_c_b, w_c_out, w_o, norm2_g, router_w, router_b,
              exp_w1, exp_b1, exp_w2, exp_b2, final_g):
    cos, sin = rope_tables(positions, x.dtype)
    cond = jax.nn.silu(c)
    for l in range(DEPTH):
        mod = cond @ mod_w[l] + mod_b[l]
        sh1, sc1, g1, sh2, sc2, g2 = jnp.split(mod, 6, axis=-1)
        h = modulate(rms_norm(x, norm1_g[l]), sh1, sc1)
        x = x + g1[:, None, :] * hybrid_mixer(
            h, cos, sin, w_in[l], conv_a_w[l], w_a_out[l], ret_norm_g[l], w_b_out[l], glu_b[l],
            conv_c_w[l], conv_c_b[l], ln_c_g[l], ln_c_b[l], w_c_out[l], w_o[l])
        h = modulate(rms_norm(x, norm2_g[l]), sh2, sc2)
        x = x + g2[:, None, :] * moe(h, router_w[l], router_b[l], exp_w1[l], exp_b1[l], exp_w2[l], exp_b2[l])
    return rms_norm(x, final_g)
```

```python
import functools

import jax
import jax.numpy as jnp
from jax import lax
from jax.experimental import pallas as pl
from jax.experimental.pallas import tpu as pltpu

F32 = jnp.float32
BF16 = jnp.bfloat16

EPS = 1e-6
SHORT_CONV_W = 3
RET_HEADS = 8
RET_HEAD_DIM = 128
RET_CHUNK = 128
ROPE_BASE = 10000.0
CONF_KERNEL = 31
N_EXPERTS = 32
TOP_K = 4
SWIGLU_LIMIT = 7.0
SWIGLU_ALPHA = 1.702

LANES = 128
SUBLANES = 8
VMEM_LIMIT = 56 * 1024 * 1024

ROW_TILE = 512
EXPERT_BLOCK = 256
COMBINE_TILE = 256
CONV_ROWS = 32
CONF_HALO = 32


def _params(n_grid):
    return pltpu.CompilerParams(
        dimension_semantics=("arbitrary",) * n_grid, vmem_limit_bytes=VMEM_LIMIT)


def _const_spec(shape):
    nd = len(shape)
    return pl.BlockSpec(shape, lambda *_: (0,) * nd, pipeline_mode=pl.Buffered(1))


def _sigmoid(x):
    return 1.0 / (1.0 + jnp.exp(-x))


def _silu(x):
    return x * _sigmoid(x)


def _norm_mod(x, g, sh, sc):
    y = x * lax.rsqrt(jnp.mean(x * x, axis=-1, keepdims=True) + EPS)
    return (y * g) * (1.0 + sc) + sh


def _bdot(a, b):
    return jnp.dot(a.astype(BF16), b.astype(BF16), preferred_element_type=F32)


def _mod_kernel(c_ref, w_ref, b_ref, o_ref):
    cond = _silu(c_ref[...])
    o_ref[0] = jnp.dot(cond, w_ref[0], preferred_element_type=F32,
                       precision=lax.Precision.HIGHEST) + b_ref[0]


def _modulation(c, mod_w, mod_b):
    L, D, M = mod_w.shape
    B = c.shape[0]
    tn = 1024
    return pl.pallas_call(
        _mod_kernel,
        grid=(L, M // tn),
        in_specs=[pl.BlockSpec((B, D), lambda l, j: (0, 0)),
                  pl.BlockSpec((1, D, tn), lambda l, j: (l, 0, j)),
                  pl.BlockSpec((1, 1, tn), lambda l, j: (l, 0, j))],
        out_specs=pl.BlockSpec((1, B, tn), lambda l, j: (l, 0, j)),
        out_shape=jax.ShapeDtypeStruct((L, B, M), F32),
        compiler_params=_params(2),
        name="modulation",
    )(c, mod_w, mod_b.reshape(L, 1, M))


def _rope_kernel(pos_ref, invf_ref, cos_ref, sin_ref):
    ang = pos_ref[0].astype(F32) * invf_ref[...]
    lane = lax.broadcasted_iota(jnp.int32, ang.shape, 1)
    s = jnp.sin(ang)
    cos_ref[0] = jnp.cos(ang)
    sin_ref[0] = jnp.where(lane < RET_HEAD_DIM // 2, -s, s)


def _rope_tables(positions):
    B, S = positions.shape
    inv_freq = 1.0 / (ROPE_BASE ** (jnp.arange(0, RET_HEAD_DIM, 2, dtype=F32) / RET_HEAD_DIM))
    invf = jnp.concatenate([inv_freq, inv_freq]).reshape(1, RET_HEAD_DIM)
    ts = min(S, 1024)
    spec = pl.BlockSpec((1, ts, RET_HEAD_DIM), lambda b, j: (b, j, 0))
    return pl.pallas_call(
        _rope_kernel,
        grid=(B, S // ts),
        in_specs=[pl.BlockSpec((1, ts, 1), lambda b, j: (b, j, 0)),
                  pl.BlockSpec((1, RET_HEAD_DIM), lambda b, j: (0, 0))],
        out_specs=[spec, spec],
        out_shape=[jax.ShapeDtypeStruct((B, S, RET_HEAD_DIM), F32)] * 2,
        compiler_params=_params(2),
        name="rope_tables",
    )(positions.reshape(B, S, 1), invf)


def _branch_a_kernel(x_ref, g_ref, sh_ref, sc_ref, w_ref, wg_ref, cw_ref, wo_ref, o_ref, buf_ref):
    tm = x_ref.shape[1]
    da = wo_ref.shape[0]

    @pl.when(pl.program_id(1) == 0)
    def _():
        buf_ref[0:SUBLANES, :] = jnp.zeros((SUBLANES, da), F32)

    h = _norm_mod(x_ref[0], g_ref[...], sh_ref[0], sc_ref[0]).astype(BF16)
    p = jnp.dot(h, w_ref[...], preferred_element_type=F32)
    b_g, c_g, xa = p[:, :da], p[:, da:2 * da], p[:, 2 * da:]
    u = c_g * xa
    buf_ref[SUBLANES:SUBLANES + tm, :] = u
    conv = cw_ref[2:3, :] * u
    for j in range(SHORT_CONV_W - 1):
        off = SUBLANES - (SHORT_CONV_W - 1) + j
        conv = conv + cw_ref[j:j + 1, :] * buf_ref[off:off + tm, :]
    buf_ref[0:SUBLANES, :] = buf_ref[tm:tm + SUBLANES, :]
    y_a = _bdot(b_g * conv, wo_ref[...])
    gate = _sigmoid(jnp.dot(h, wg_ref[...], preferred_element_type=F32))
    o_ref[0] = (gate * y_a).astype(BF16)


def _branch_a(x, g, sh, sc, w, wg, cw, wo):
    B, S, D = x.shape
    da = wo.shape[0]
    tm = min(ROW_TILE, S)
    vec = pl.BlockSpec((1, 1, D), lambda b, j: (b, 0, 0))
    return pl.pallas_call(
        _branch_a_kernel,
        grid=(B, S // tm),
        in_specs=[pl.BlockSpec((1, tm, D), lambda b, j: (b, j, 0)),
                  _const_spec((1, D)), vec, vec,
                  _const_spec(w.shape), _const_spec(wg.shape), _const_spec(cw.shape),
                  _const_spec(wo.shape)],
        out_specs=pl.BlockSpec((1, tm, D), lambda b, j: (b, j, 0)),
        out_shape=jax.ShapeDtypeStruct((B, S, D), BF16),
        scratch_shapes=[pltpu.VMEM((tm + SUBLANES, da), F32)],
        compiler_params=_params(2),
        name="branch_a",
    )(x, g, sh, sc, w, wg, cw, wo)


def _branch_c_kernel(x_ref, g_ref, sh_ref, sc_ref, w_ref, wg_ref, glub_ref, cw_ref, cb_ref,
                     lng_ref, lnb_ref, wo_ref, o_ref, buf_ref, conv_ref):
    tm = x_ref.shape[1]
    dc = wo_ref.shape[0]

    @pl.when(pl.program_id(1) == 0)
    def _():
        buf_ref[0, 0:CONF_HALO, :] = jnp.zeros((CONF_HALO, dc), F32)

    h = _norm_mod(x_ref[0], g_ref[...], sh_ref[0], sc_ref[0]).astype(BF16)
    p = jnp.dot(h, w_ref[...], preferred_element_type=F32) + glub_ref[...]
    buf_ref[0, CONF_HALO:CONF_HALO + tm, :] = p[:, :dc] * _sigmoid(p[:, dc:])
    for b in range(1, SUBLANES):
        buf_ref[b, SUBLANES:, :] = buf_ref[0, SUBLANES - b:tm + CONF_HALO - b, :]

    base = CONF_HALO - (CONF_KERNEL - 1)

    def strip(s, carry):
        r0 = pl.multiple_of(s * CONV_ROWS, CONV_ROWS)
        acc = jnp.zeros((CONV_ROWS, dc), F32)
        for j in range(CONF_KERNEL):
            a = -(-(base + j) // SUBLANES)
            b = a * SUBLANES - (base + j)
            acc = acc + cw_ref[j:j + 1, :] * buf_ref[b, pl.ds(r0 + a * SUBLANES, CONV_ROWS), :]
        conv_ref[pl.ds(r0, CONV_ROWS), :] = acc
        return carry

    lax.fori_loop(0, tm // CONV_ROWS, strip, 0)
    buf_ref[0, 0:CONF_HALO, :] = buf_ref[0, tm:tm + CONF_HALO, :]

    u = conv_ref[...] + cb_ref[...]
    mu = jnp.mean(u, axis=-1, keepdims=True)
    d = u - mu
    var = jnp.mean(d * d, axis=-1, keepdims=True)
    y = d * lax.rsqrt(var + EPS) * lng_ref[...] + lnb_ref[...]
    y_c = _bdot(_silu(y), wo_ref[...])
    gate = _sigmoid(jnp.dot(h, wg_ref[...], preferred_element_type=F32))
    o_ref[0] = (gate * y_c).astype(BF16)


def _branch_c(x, g, sh, sc, w, wg, glub, cw, cb, lng, lnb, wo):
    B, S, D = x.shape
    dc = wo.shape[0]
    tm = min(ROW_TILE, S)
    vec = pl.BlockSpec((1, 1, D), lambda b, j: (b, 0, 0))
    return pl.pallas_call(
        _branch_c_kernel,
        grid=(B, S // tm),
        in_specs=[pl.BlockSpec((1, tm, D), lambda b, j: (b, j, 0)),
                  _const_spec((1, D)), vec, vec,
                  _const_spec(w.shape), _const_spec(wg.shape), _const_spec(glub.shape),
                  _const_spec(cw.shape), _const_spec(cb.shape), _const_spec(lng.shape),
                  _const_spec(lnb.shape), _const_spec(wo.shape)],
        out_specs=pl.BlockSpec((1, tm, D), lambda b, j: (b, j, 0)),
        out_shape=jax.ShapeDtypeStruct((B, S, D), BF16),
        scratch_shapes=[pltpu.VMEM((SUBLANES, tm + CONF_HALO, dc), F32), pltpu.VMEM((tm, dc), F32)],
        compiler_params=_params(2),
        name="branch_c",
    )(x, g, sh, sc, w, wg, glub, cw, cb, lng, lnb, wo)


def _branch_b_kernel(x_ref, g_ref, sh_ref, sc_ref, cos_ref, sin_ref, w_ref, wg_ref,
                     dmask_ref, xi_ref, zeta_ref, gch_ref, rg_ref, wo_ref, o_ref,
                     q_ref, k_ref, v_ref, r_ref, state_ref):
    tm = x_ref.shape[1]
    dr = wo_ref.shape[0]
    hd = RET_HEAD_DIM

    @pl.when(pl.program_id(1) == 0)
    def _():
        state_ref[...] = jnp.zeros(state_ref.shape, F32)

    h = _norm_mod(x_ref[0], g_ref[...], sh_ref[0], sc_ref[0]).astype(BF16)
    cos = cos_ref[0]
    sin = sin_ref[0]
    scale = hd ** -0.5

    def rope(t):
        return t * cos + pltpu.roll(t, hd // 2, axis=1) * sin

    q = jnp.dot(h, w_ref[:, 0:dr], preferred_element_type=F32) * scale
    k = jnp.dot(h, w_ref[:, dr:2 * dr], preferred_element_type=F32)
    for hh in range(RET_HEADS):
        sl = slice(hh * hd, (hh + 1) * hd)
        q_ref[:, sl] = rope(q[:, sl]).astype(BF16)
        k_ref[:, sl] = rope(k[:, sl])
    v_ref[...] = jnp.dot(h, w_ref[:, 2 * dr:3 * dr], preferred_element_type=F32).astype(BF16)

    def chunk(ci, carry):
        r0 = pl.multiple_of(ci * RET_CHUNK, RET_CHUNK)
        rows = pl.ds(r0, RET_CHUNK)
        for hh in range(RET_HEADS):
            sl = slice(hh * hd, (hh + 1) * hd)
            qh = q_ref[rows, sl]
            kh = k_ref[rows, sl]
            vh = v_ref[rows, sl]
            st = state_ref[hh]
            scores = lax.dot_general(qh, kh.astype(BF16), (((1,), (1,)), ((), ())),
                                     preferred_element_type=F32) * dmask_ref[hh]
            inner = _bdot(scores, vh)
            cross = _bdot(qh, st) * xi_ref[hh]
            kz = (kh * zeta_ref[hh]).astype(BF16)
            state_ref[hh] = st * gch_ref[hh] + lax.dot_general(
                kz, vh, (((0,), (0,)), ((), ())), preferred_element_type=F32)
            o = inner + cross
            mu = jnp.mean(o, axis=-1, keepdims=True)
            d = o - mu
            var = jnp.mean(d * d, axis=-1, keepdims=True)
            r_ref[rows, sl] = d * lax.rsqrt(var + EPS)
        return carry

    lax.fori_loop(0, tm // RET_CHUNK, chunk, 0)

    og = jnp.dot(h, w_ref[:, 3 * dr:4 * dr], preferred_element_type=F32)
    y_b = _bdot(_silu(og) * (r_ref[...] * rg_ref[...]), wo_ref[...])
    gate = _sigmoid(jnp.dot(h, wg_ref[...], preferred_element_type=F32))
    o_ref[0] = (gate * y_b).astype(BF16)


def _retention_tables():
    H, C = RET_HEADS, RET_CHUNK
    gamma = 1.0 - jnp.power(2.0, -5.0 - jnp.arange(H, dtype=F32))
    lg = jnp.log(gamma)
    idx = jnp.arange(C, dtype=F32)
    rel = idx[:, None] - idx[None, :]
    dmask = jnp.where(rel[None] >= 0, jnp.exp(jnp.maximum(rel, 0.0)[None] * lg[:, None, None]), 0.0)
    xi = jnp.exp((idx + 1.0)[None] * lg[:, None])
    zeta = jnp.exp((C - 1.0 - idx)[None] * lg[:, None])
    g_chunk = jnp.exp(C * lg)
    bc = lambda t: jnp.broadcast_to(t[:, :, None], (H, C, RET_HEAD_DIM)).astype(F32)
    g_rows = jnp.broadcast_to(g_chunk[:, None, None], (H, 1, RET_HEAD_DIM)).astype(F32)
    return dmask.astype(F32), bc(xi), bc(zeta), g_rows


def _branch_b(x, g, sh, sc, cosf, sinf, w, wg, tables, rg, wo):
    B, S, D = x.shape
    dr = wo.shape[0]
    tm = min(ROW_TILE, S)
    dmask, xi, zeta, g_chunk = tables
    vec = pl.BlockSpec((1, 1, D), lambda b, j: (b, 0, 0))
    rope_spec = pl.BlockSpec((1, tm, RET_HEAD_DIM), lambda b, j: (b, j, 0))
    return pl.pallas_call(
        _branch_b_kernel,
        grid=(B, S // tm),
        in_specs=[pl.BlockSpec((1, tm, D), lambda b, j: (b, j, 0)),
                  _const_spec((1, D)), vec, vec, rope_spec, rope_spec,
                  _const_spec(w.shape), _const_spec(wg.shape), _const_spec(dmask.shape),
                  _const_spec(xi.shape), _const_spec(zeta.shape), _const_spec(g_chunk.shape),
                  _const_spec(rg.shape), _const_spec(wo.shape)],
        out_specs=pl.BlockSpec((1, tm, D), lambda b, j: (b, j, 0)),
        out_shape=jax.ShapeDtypeStruct((B, S, D), BF16),
        scratch_shapes=[pltpu.VMEM((tm, dr), BF16), pltpu.VMEM((tm, dr), F32),
                        pltpu.VMEM((tm, dr), BF16), pltpu.VMEM((tm, dr), F32),
                        pltpu.VMEM((RET_HEADS, RET_HEAD_DIM, RET_HEAD_DIM), F32)],
        compiler_params=_params(2),
        name="branch_b",
    )(x, g, sh, sc, cosf, sinf, w, wg, dmask, xi, zeta, g_chunk, rg, wo)


def _out_router_kernel(ma_ref, mb_ref, mc_ref, x_ref, g1_ref, wo_ref, n2_ref, sh_ref, sc_ref,
                       rw_ref, rb_ref, x1_ref, h2_ref, idx_ref, prob_ref, rank_ref, cnt_ref,
                       carry_ref):
    tm = x_ref.shape[1]
    first = jnp.logical_and(pl.program_id(0) == 0, pl.program_id(1) == 0)

    @pl.when(first)
    def _():
        carry_ref[...] = jnp.zeros(carry_ref.shape, F32)

    m = ma_ref[0].astype(F32) + mb_ref[0].astype(F32) + mc_ref[0].astype(F32)
    x1 = x_ref[0] + g1_ref[0] * _bdot(m, wo_ref[...])
    x1_ref[0] = x1
    h2 = _norm_mod(x1, n2_ref[...], sh_ref[0], sc_ref[0])
    h2_ref[0] = h2

    logits = jnp.dot(h2, rw_ref[...], preferred_element_type=F32,
                     precision=lax.Precision.HIGHEST) + rb_ref[...]
    lane = lax.broadcasted_iota(jnp.int32, logits.shape, 1)
    lane_f = lane.astype(F32)
    neg = jnp.float32(-jnp.inf)
    work = jnp.where(lane < N_EXPERTS, logits, neg)
    idx_out = jnp.zeros(logits.shape, jnp.int32)
    val_out = jnp.full(logits.shape, neg, F32)
    onehot_all = jnp.zeros(logits.shape, F32)
    sels = []
    for kk in range(TOP_K):
        mx = jnp.max(work, axis=-1, keepdims=True)
        first_idx = jnp.min(jnp.where(work == mx, lane_f, float(LANES)), axis=-1, keepdims=True)
        sel = lane_f == first_idx
        sels.append(sel)
        idx_out = jnp.where(lane == kk, first_idx.astype(jnp.int32), idx_out)
        val_out = jnp.where(lane == kk, mx, val_out)
        onehot_all = onehot_all + sel.astype(F32)
        work = jnp.where(sel, neg, work)

    ex = jnp.exp(val_out - val_out[:, 0:1])
    prob_ref[0] = ex / jnp.sum(ex, axis=-1, keepdims=True)
    idx_ref[0] = idx_out

    row = lax.broadcasted_iota(jnp.int32, (tm, tm), 0)
    col = lax.broadcasted_iota(jnp.int32, (tm, tm), 1)
    lower = (col < row).astype(BF16)
    before = jnp.dot(lower, onehot_all.astype(BF16), preferred_element_type=F32) + carry_ref[...]
    rank_out = jnp.zeros(logits.shape, jnp.int32)
    for kk in range(TOP_K):
        rk = jnp.sum(jnp.where(sels[kk], before, 0.0), axis=-1, keepdims=True)
        rank_out = jnp.where(lane == kk, rk.astype(jnp.int32), rank_out)
    rank_ref[0] = rank_out
    total = carry_ref[...] + jnp.sum(onehot_all, axis=0, keepdims=True)
    carry_ref[...] = total
    cnt_ref[...] = total


def _out_router(ma, mb, mc, x, g1, wo, n2, sh2, sc2, rw, rb):
    B, S, D = x.shape
    tm = min(ROW_TILE, S)
    tile = pl.BlockSpec((1, tm, D), lambda b, j: (b, j, 0))
    vec = pl.BlockSpec((1, 1, D), lambda b, j: (b, 0, 0))
    lanes = pl.BlockSpec((1, tm, LANES), lambda b, j: (b, j, 0))
    return pl.pallas_call(
        _out_router_kernel,
        grid=(B, S // tm),
        in_specs=[tile, tile, tile, tile, vec, _const_spec(wo.shape), _const_spec((1, D)),
                  vec, vec, _const_spec(rw.shape), _const_spec(rb.shape)],
        out_specs=[tile, tile, lanes, lanes, lanes, pl.BlockSpec((1, LANES), lambda b, j: (0, 0))],
        out_shape=[jax.ShapeDtypeStruct((B, S, D), F32), jax.ShapeDtypeStruct((B, S, D), F32),
                   jax.ShapeDtypeStruct((B, S, LANES), jnp.int32),
                   jax.ShapeDtypeStruct((B, S, LANES), F32),
                   jax.ShapeDtypeStruct((B, S, LANES), jnp.int32),
                   jax.ShapeDtypeStruct((1, LANES), F32)],
        scratch_shapes=[pltpu.VMEM((1, LANES), F32)],
        compiler_params=_params(2),
        name="out_router",
    )(ma, mb, mc, x, g1, wo, n2, sh2, sc2, rw, rb)


def _row_gather(src_hbm, idx_ref, dst_ref, sem, n_rows, start):
    def body(i, carry):
        cp = pltpu.make_async_copy(src_hbm.at[pl.ds(idx_ref[0, 0, i], 1)],
                                   dst_ref.at[pl.ds(i, 1)], sem)
        if start:
            cp.start()
        else:
            cp.wait()
        return carry
    lax.fori_loop(0, n_rows, body, 0)


def _expert_kernel(be_ref, bv_ref, cur_idx_ref, nxt_idx_ref, h_hbm, w1_ref, b1_ref, w2_ref,
                   b2_ref, o_ref, xbuf_ref, sem_ref):
    i = pl.program_id(0)
    nb = pl.num_programs(0)
    blk = o_ref.shape[0]
    f = w2_ref.shape[1]
    slot = lax.rem(i, 2)

    @pl.when(jnp.logical_and(i == 0, bv_ref[0] == 1))
    def _():
        _row_gather(h_hbm, cur_idx_ref, xbuf_ref.at[0], sem_ref.at[0], blk, True)

    nxt = jnp.minimum(i + 1, nb - 1)

    @pl.when(jnp.logical_and(i + 1 < nb, bv_ref[nxt] == 1))
    def _():
        _row_gather(h_hbm, nxt_idx_ref, xbuf_ref.at[1 - slot], sem_ref.at[1 - slot], blk, True)

    @pl.when(bv_ref[i] == 1)
    def _():
        _row_gather(h_hbm, cur_idx_ref, xbuf_ref.at[slot], sem_ref.at[slot], blk, False)
        xb = xbuf_ref[slot].astype(BF16)
        gu = jnp.dot(xb, w1_ref[0], preferred_element_type=F32) + b1_ref[0]
        glu = jnp.minimum(gu[:, :f], SWIGLU_LIMIT)
        lin = jnp.clip(gu[:, f:], -SWIGLU_LIMIT, SWIGLU_LIMIT)
        act = glu * _sigmoid(SWIGLU_ALPHA * glu) * (lin + 1.0)
        o_ref[...] = _bdot(act, w2_ref[0]) + b2_ref[0]

    @pl.when(bv_ref[i] == 0)
    def _():
        o_ref[...] = jnp.zeros(o_ref.shape, F32)


def _experts(h2, tok_of_slot, blk_expert, blk_valid, w1, b1, w2, b2):
    N, D = h2.shape
    E, _, F2 = w1.shape
    nb = blk_expert.shape[0]
    blk = EXPERT_BLOCK
    idx3 = tok_of_slot.reshape(nb, 1, blk)
    smem_blk = lambda fn: pl.BlockSpec((1, 1, blk), fn, memory_space=pltpu.SMEM)
    grid_spec = pltpu.PrefetchScalarGridSpec(
        num_scalar_prefetch=2,
        grid=(nb,),
        in_specs=[smem_blk(lambda i, be, bv: (i, 0, 0)),
                  smem_blk(lambda i, be, bv: (jnp.minimum(i + 1, nb - 1), 0, 0)),
                  pl.BlockSpec(memory_space=pl.ANY),
                  pl.BlockSpec((1, D, F2), lambda i, be, bv: (be[i], 0, 0)),
                  pl.BlockSpec((1, 1, F2), lambda i, be, bv: (be[i], 0, 0)),
                  pl.BlockSpec((1, F2 // 2, D), lambda i, be, bv: (be[i], 0, 0)),
                  pl.BlockSpec((1, 1, D), lambda i, be, bv: (be[i], 0, 0))],
        out_specs=pl.BlockSpec((blk, D), lambda i, be, bv: (i, 0)),
        scratch_shapes=[pltpu.VMEM((2, blk, D), F32), pltpu.SemaphoreType.DMA((2,))])
    return pl.pallas_call(
        _expert_kernel,
        grid_spec=grid_spec,
        out_shape=jax.ShapeDtypeStruct((nb * blk, D), F32),
        compiler_params=_params(1),
        name="experts",
    )(blk_expert, blk_valid, idx3, idx3, h2, w1, b1.reshape(E, 1, F2), w2, b2.reshape(E, 1, D))


def _combine_kernel(cur_idx_ref, nxt_idx_ref, y_hbm, x1_ref, g2_ref, prob_ref, fg_ref, o_ref,
                    ybuf_ref, sem_ref, *, final_norm):
    i = pl.program_id(0)
    nt = pl.num_programs(0)
    tm = x1_ref.shape[0]
    slot = lax.rem(i, 2)
    n_rows = TOP_K * tm

    @pl.when(i == 0)
    def _():
        _row_gather(y_hbm, cur_idx_ref, ybuf_ref.at[0], sem_ref.at[0], n_rows, True)

    @pl.when(i + 1 < nt)
    def _():
        _row_gather(y_hbm, nxt_idx_ref, ybuf_ref.at[1 - slot], sem_ref.at[1 - slot], n_rows, True)

    _row_gather(y_hbm, cur_idx_ref, ybuf_ref.at[slot], sem_ref.at[slot], n_rows, False)
    prob = prob_ref[...]
    acc = jnp.zeros(x1_ref.shape, F32)
    for kk in range(TOP_K):
        acc = acc + prob[:, kk:kk + 1] * ybuf_ref[slot, kk * tm:(kk + 1) * tm, :]
    x2 = x1_ref[...] + g2_ref[0] * acc
    if final_norm:
        x2 = x2 * lax.rsqrt(jnp.mean(x2 * x2, axis=-1, keepdims=True) + EPS) * fg_ref[...]
    o_ref[...] = x2


def _combine(ybuf, dest, x1, g2, prob, final_g, seq_len, final_norm):
    N, D = x1.shape
    tm = min(COMBINE_TILE, seq_len)
    nt = N // tm
    per_seq = seq_len // tm
    idx3 = dest.reshape(nt, tm, TOP_K).transpose(0, 2, 1).reshape(nt, 1, TOP_K * tm)
    smem_blk = lambda fn: pl.BlockSpec((1, 1, TOP_K * tm), fn, memory_space=pltpu.SMEM)
    return pl.pallas_call(
        functools.partial(_combine_kernel, final_norm=final_norm),
        grid=(nt,),
        in_specs=[smem_blk(lambda i: (i, 0, 0)),
                  smem_blk(lambda i: (jnp.minimum(i + 1, nt - 1), 0, 0)),
                  pl.BlockSpec(memory_space=pl.ANY),
                  pl.BlockSpec((tm, D), lambda i: (i, 0)),
                  pl.BlockSpec((1, 1, D), lambda i: (i // per_seq, 0, 0)),
                  pl.BlockSpec((tm, LANES), lambda i: (i, 0)),
                  pl.BlockSpec((1, D), lambda i: (0, 0))],
        out_specs=pl.BlockSpec((tm, D), lambda i: (i, 0)),
        out_shape=jax.ShapeDtypeStruct((N, D), F32),
        scratch_shapes=[pltpu.VMEM((2, TOP_K * tm, D), F32), pltpu.SemaphoreType.DMA((2,))],
        compiler_params=_params(1),
        name="combine",
    )(idx3, idx3, ybuf, x1, g2, prob, final_g)


def _routing_tables(counts, idx, rank, n_tokens):
    blk = EXPERT_BLOCK
    A = n_tokens * TOP_K
    nb = (A + N_EXPERTS * (blk - 1)) // blk
    counts = counts.astype(jnp.int32)
    padded = ((counts + blk - 1) // blk) * blk
    pends = jnp.cumsum(padded)
    pstarts = pends - padded
    block_start = jnp.arange(nb, dtype=jnp.int32) * blk
    blk_expert = jnp.minimum(jnp.searchsorted(pends, block_start, side='right'),
                             N_EXPERTS - 1).astype(jnp.int32)
    blk_valid = (block_start < pends[-1]).astype(jnp.int32)
    dest = pstarts[idx] + rank
    tok = jnp.repeat(jnp.arange(n_tokens, dtype=jnp.int32), TOP_K)
    tok_of_slot = jnp.zeros((nb * blk,), jnp.int32).at[dest.reshape(A)].set(tok)
    return dest, tok_of_slot, blk_expert, blk_valid


def kernel(x, c, positions, mod_w, mod_b, norm1_g, w_in, conv_a_w, w_a_out, ret_norm_g, w_b_out, glu_b, conv_c_w, conv_c_b, ln_c_g, ln_c_b, w_c_out, w_o, norm2_g, router_w, router_b, exp_w1, exp_b1, exp_w2, exp_b2, final_g):
    B, S, D = x.shape
    L = mod_w.shape[0]
    N = B * S
    da = w_a_out.shape[1]
    dr = w_b_out.shape[1]
    dc = w_c_out.shape[1]
    s1 = 3 * da
    s2 = s1 + 4 * dr
    s3 = s2 + 2 * dc

    mod = _modulation(c, mod_w, mod_b)
    cosf, sinf = _rope_tables(positions)
    tables = _retention_tables()
    row = lambda v: v.reshape(1, -1)
    rw_pad = jnp.zeros((L, D, LANES), F32).at[:, :, :N_EXPERTS].set(router_w)
    rb_pad = jnp.zeros((L, 1, LANES), F32).at[:, 0, :N_EXPERTS].set(router_b)

    for l in range(L):
        sh1, sc1, g1, sh2, sc2, g2 = [mod[l, :, i * D:(i + 1) * D].reshape(B, 1, D) for i in range(6)]
        wl = w_in[l].astype(BF16)
        wg = wl[:, s3:]
        n1 = row(norm1_g[l])
        ma = _branch_a(x, n1, sh1, sc1, wl[:, :s1], wg[:, :D], conv_a_w[l],
                       w_a_out[l].astype(BF16))
        mb = _branch_b(x, n1, sh1, sc1, cosf, sinf, wl[:, s1:s2], wg[:, D:2 * D], tables,
                       row(ret_norm_g[l]), w_b_out[l].astype(BF16))
        mc = _branch_c(x, n1, sh1, sc1, wl[:, s2:s3], wg[:, 2 * D:], row(glu_b[l]), conv_c_w[l],
                       row(conv_c_b[l]), row(ln_c_g[l]), row(ln_c_b[l]), w_c_out[l].astype(BF16))
        x1, h2, idx, prob, rank, counts = _out_router(
            ma, mb, mc, x, g1, w_o[l].astype(BF16), row(norm2_g[l]), sh2, sc2, rw_pad[l], rb_pad[l])
        idx = idx.reshape(N, LANES)[:, :TOP_K]
        rank = rank.reshape(N, LANES)[:, :TOP_K]
        dest, tok_of_slot, blk_expert, blk_valid = _routing_tables(
            counts[0, :N_EXPERTS], idx, rank, N)
        ybuf = _experts(h2.reshape(N, D), tok_of_slot, blk_expert, blk_valid,
                        exp_w1[l].astype(BF16), exp_b1[l], exp_w2[l].astype(BF16), exp_b2[l])
        x = _combine(ybuf, dest, x1.reshape(N, D), g2, prob.reshape(N, LANES), row(final_g),
                     S, final_norm=(l == L - 1)).reshape(B, S, D)
    return x
```

```python
import functools

import jax
import jax.numpy as jnp
from jax import lax
from jax.experimental import pallas as pl
from jax.experimental.pallas import tpu as pltpu

F32 = jnp.float32
BF16 = jnp.bfloat16

EPS = 1e-6
SHORT_CONV_W = 3
RET_HEADS = 8
RET_HEAD_DIM = 128
RET_CHUNK = 128
ROPE_BASE = 10000.0
CONF_KERNEL = 31
N_EXPERTS = 32
TOP_K = 4
SWIGLU_LIMIT = 7.0
SWIGLU_ALPHA = 1.702

LANES = 128
SUBLANES = 8
VMEM_LIMIT = 56 * 1024 * 1024

ROW_TILE = 512
EXPERT_BLOCK = 256
COMBINE_TILE = 256
CONV_ROWS = 32
CONF_HALO = 32


def _params(n_grid):
    return pltpu.CompilerParams(
        dimension_semantics=("arbitrary",) * n_grid, vmem_limit_bytes=VMEM_LIMIT)


def _const_spec(shape):
    nd = len(shape)
    return pl.BlockSpec(shape, lambda *_: (0,) * nd, pipeline_mode=pl.Buffered(1))


def _sigmoid(x):
    return 1.0 / (1.0 + jnp.exp(-x))


def _silu(x):
    return x * _sigmoid(x)


def _norm_mod(x, g, sh, sc):
    y = x * lax.rsqrt(jnp.mean(x * x, axis=-1, keepdims=True) + EPS)
    return (y * g) * (1.0 + sc) + sh


def _bdot(a, b):
    return jnp.dot(a.astype(BF16), b.astype(BF16), preferred_element_type=F32)


def _mod_kernel(c_ref, w_ref, b_ref, o_ref):
    cond = _silu(c_ref[...])
    o_ref[0] = jnp.dot(cond, w_ref[0], preferred_element_type=F32,
                       precision=lax.Precision.HIGHEST) + b_ref[0]


def _modulation(c, mod_w, mod_b):
    L, D, M = mod_w.shape
    B = c.shape[0]
    tn = 1024
    return pl.pallas_call(
        _mod_kernel,
        grid=(L, M // tn),
        in_specs=[pl.BlockSpec((B, D), lambda l, j: (0, 0)),
                  pl.BlockSpec((1, D, tn), lambda l, j: (l, 0, j)),
                  pl.BlockSpec((1, 1, tn), lambda l, j: (l, 0, j))],
        out_specs=pl.BlockSpec((1, B, tn), lambda l, j: (l, 0, j)),
        out_shape=jax.ShapeDtypeStruct((L, B, M), F32),
        compiler_params=_params(2),
        name="modulation",
    )(c, mod_w, mod_b.reshape(L, 1, M))


def _rope_kernel(pos_ref, invf_ref, cos_ref, sin_ref):
    ang = pos_ref[0].astype(F32) * invf_ref[...]
    lane = lax.broadcasted_iota(jnp.int32, ang.shape, 1)
    s = jnp.sin(ang)
    cos_ref[0] = jnp.cos(ang)
    sin_ref[0] = jnp.where(lane < RET_HEAD_DIM // 2, -s, s)


def _rope_tables(positions):
    B, S = positions.shape
    inv_freq = 1.0 / (ROPE_BASE ** (jnp.arange(0, RET_HEAD_DIM, 2, dtype=F32) / RET_HEAD_DIM))
    invf = jnp.concatenate([inv_freq, inv_freq]).reshape(1, RET_HEAD_DIM)
    ts = min(S, 1024)
    spec = pl.BlockSpec((1, ts, RET_HEAD_DIM), lambda b, j: (b, j, 0))
    return pl.pallas_call(
        _rope_kernel,
        grid=(B, S // ts),
        in_specs=[pl.BlockSpec((1, ts, 1), lambda b, j: (b, j, 0)),
                  pl.BlockSpec((1, RET_HEAD_DIM), lambda b, j: (0, 0))],
        out_specs=[spec, spec],
        out_shape=[jax.ShapeDtypeStruct((B, S, RET_HEAD_DIM), F32)] * 2,
        compiler_params=_params(2),
        name="rope_tables",
    )(positions.reshape(B, S, 1), invf)


def _branch_a_kernel(x_ref, g_ref, sh_ref, sc_ref, w_ref, wg_ref, cw_ref, wo_ref, o_ref, buf_ref):
    tm = x_ref.shape[1]
    da = wo_ref.shape[0]

    @pl.when(pl.program_id(1) == 0)
    def _():
        buf_ref[0:SUBLANES, :] = jnp.zeros((SUBLANES, da), F32)

    h = _norm_mod(x_ref[0], g_ref[...], sh_ref[0], sc_ref[0]).astype(BF16)
    p = jnp.dot(h, w_ref[...], preferred_element_type=F32)
    b_g, c_g, xa = p[:, :da], p[:, da:2 * da], p[:, 2 * da:]
    u = c_g * xa
    buf_ref[SUBLANES:SUBLANES + tm, :] = u
    conv = cw_ref[2:3, :] * u
    for j in range(SHORT_CONV_W - 1):
        off = SUBLANES - (SHORT_CONV_W - 1) + j
        conv = conv + cw_ref[j:j + 1, :] * buf_ref[off:off + tm, :]
    buf_ref[0:SUBLANES, :] = buf_ref[tm:tm + SUBLANES, :]
    y_a = _bdot(b_g * conv, wo_ref[...])
    gate = _sigmoid(jnp.dot(h, wg_ref[...], preferred_element_type=F32))
    o_ref[0] = (gate * y_a).astype(BF16)


def _branch_a(x, g, sh, sc, w, wg, cw, wo):
    B, S, D = x.shape
    da = wo.shape[0]
    tm = min(ROW_TILE, S)
    vec = pl.BlockSpec((1, 1, D), lambda b, j: (b, 0, 0))
    return pl.pallas_call(
        _branch_a_kernel,
        grid=(B, S // tm),
        in_specs=[pl.BlockSpec((1, tm, D), lambda b, j: (b, j, 0)),
                  _const_spec((1, D)), vec, vec,
                  _const_spec(w.shape), _const_spec(wg.shape), _const_spec(cw.shape),
                  _const_spec(wo.shape)],
        out_specs=pl.BlockSpec((1, tm, D), lambda b, j: (b, j, 0)),
        out_shape=jax.ShapeDtypeStruct((B, S, D), BF16),
        scratch_shapes=[pltpu.VMEM((tm + SUBLANES, da), F32)],
        compiler_params=_params(2),
        name="branch_a",
    )(x, g, sh, sc, w, wg, cw, wo)


def _branch_c_kernel(x_ref, g_ref, sh_ref, sc_ref, w_ref, wg_ref, glub_ref, cw_ref, cb_ref,
                     lng_ref, lnb_ref, wo_ref, o_ref, buf_ref, conv_ref):
    tm = x_ref.shape[1]
    dc = wo_ref.shape[0]

    @pl.when(pl.program_id(1) == 0)
    def _():
        buf_ref[0, 0:CONF_HALO, :] = jnp.zeros((CONF_HALO, dc), F32)

    h = _norm_mod(x_ref[0], g_ref[...], sh_ref[0], sc_ref[0]).astype(BF16)
    p = jnp.dot(h, w_ref[...], preferred_element_type=F32) + glub_ref[...]
    buf_ref[0, CONF_HALO:CONF_HALO + tm, :] = p[:, :dc] * _sigmoid(p[:, dc:])
    for b in range(1, SUBLANES):
        buf_ref[b, SUBLANES:, :] = buf_ref[0, SUBLANES - b:tm + CONF_HALO - b, :]

    base = CONF_HALO - (CONF_KERNEL - 1)

    def strip(s, carry):
        r0 = pl.multiple_of(s * CONV_ROWS, CONV_ROWS)
        acc = jnp.zeros((CONV_ROWS, dc), F32)
        for j in range(CONF_KERNEL):
            a = -(-(base + j) // SUBLANES)
            b = a * SUBLANES - (base + j)
            acc = acc + cw_ref[j:j + 1, :] * buf_ref[b, pl.ds(r0 + a * SUBLANES, CONV_ROWS), :]
        conv_ref[pl.ds(r0, CONV_ROWS), :] = acc
        return carry

    lax.fori_loop(0, tm // CONV_ROWS, strip, 0)
    buf_ref[0, 0:CONF_HALO, :] = buf_ref[0, tm:tm + CONF_HALO, :]

    u = conv_ref[...] + cb_ref[...]
    mu = jnp.mean(u, axis=-1, keepdims=True)
    d = u - mu
    var = jnp.mean(d * d, axis=-1, keepdims=True)
    y = d * lax.rsqrt(var + EPS) * lng_ref[...] + lnb_ref[...]
    y_c = _bdot(_silu(y), wo_ref[...])
    gate = _sigmoid(jnp.dot(h, wg_ref[...], preferred_element_type=F32))
    o_ref[0] = (gate * y_c).astype(BF16)


def _branch_c(x, g, sh, sc, w, wg, glub, cw, cb, lng, lnb, wo):
    B, S, D = x.shape
    dc = wo.shape[0]
    tm = min(ROW_TILE, S)
    vec = pl.BlockSpec((1, 1, D), lambda b, j: (b, 0, 0))
    return pl.pallas_call(
        _branch_c_kernel,
        grid=(B, S // tm),
        in_specs=[pl.BlockSpec((1, tm, D), lambda b, j: (b, j, 0)),
                  _const_spec((1, D)), vec, vec,
                  _const_spec(w.shape), _const_spec(wg.shape), _const_spec(glub.shape),
                  _const_spec(cw.shape), _const_spec(cb.shape), _const_spec(lng.shape),
                  _const_spec(lnb.shape), _const_spec(wo.shape)],
        out_specs=pl.BlockSpec((1, tm, D), lambda b, j: (b, j, 0)),
        out_shape=jax.ShapeDtypeStruct((B, S, D), BF16),
        scratch_shapes=[pltpu.VMEM((SUBLANES, tm + CONF_HALO, dc), F32), pltpu.VMEM((tm, dc), F32)],
        compiler_params=_params(2),
        name="branch_c",
    )(x, g, sh, sc, w, wg, glub, cw, cb, lng, lnb, wo)


def _branch_b_kernel(x_ref, g_ref, sh_ref, sc_ref, cos_ref, sin_ref, w_ref, wg_ref,
                     dmask_ref, xi_ref, zeta_ref, gch_ref, rg_ref, wo_ref, o_ref,
                     q_ref, k_ref, v_ref, r_ref, state_ref):
    tm = x_ref.shape[1]
    dr = wo_ref.shape[0]
    hd = RET_HEAD_DIM

    @pl.when(pl.program_id(1) == 0)
    def _():
        state_ref[...] = jnp.zeros(state_ref.shape, F32)

    h = _norm_mod(x_ref[0], g_ref[...], sh_ref[0], sc_ref[0]).astype(BF16)
    cos = cos_ref[0]
    sin = sin_ref[0]
    scale = hd ** -0.5

    def rope(t):
        return t * cos + pltpu.roll(t, hd // 2, axis=1) * sin

    q = jnp.dot(h, w_ref[:, 0:dr], preferred_element_type=F32) * scale
    k = jnp.dot(h, w_ref[:, dr:2 * dr], preferred_element_type=F32)
    for hh in range(RET_HEADS):
        sl = slice(hh * hd, (hh + 1) * hd)
        q_ref[:, sl] = rope(q[:, sl]).astype(BF16)
        k_ref[:, sl] = rope(k[:, sl])
    v_ref[...] = jnp.dot(h, w_ref[:, 2 * dr:3 * dr], preferred_element_type=F32).astype(BF16)

    def chunk(ci, carry):
        r0 = pl.multiple_of(ci * RET_CHUNK, RET_CHUNK)
        rows = pl.ds(r0, RET_CHUNK)
        for hh in range(RET_HEADS):
            sl = slice(hh * hd, (hh + 1) * hd)
            qh = q_ref[rows, sl]
            kh = k_ref[rows, sl]
            vh = v_ref[rows, sl]
            st = state_ref[hh]
            scores = lax.dot_general(qh, kh.astype(BF16), (((1,), (1,)), ((), ())),
                                     preferred_element_type=F32) * dmask_ref[hh]
            inner = _bdot(scores, vh)
            cross = _bdot(qh, st) * xi_ref[hh]
            kz = (kh * zeta_ref[hh]).astype(BF16)
            state_ref[hh] = st * gch_ref[hh] + lax.dot_general(
                kz, vh, (((0,), (0,)), ((), ())), preferred_element_type=F32)
            o = inner + cross
            mu = jnp.mean(o, axis=-1, keepdims=True)
            d = o - mu
            var = jnp.mean(d * d, axis=-1, keepdims=True)
            r_ref[rows, sl] = d * lax.rsqrt(var + EPS)
        return carry

    lax.fori_loop(0, tm // RET_CHUNK, chunk, 0)

    og = jnp.dot(h, w_ref[:, 3 * dr:4 * dr], preferred_element_type=F32)
    y_b = _bdot(_silu(og) * (r_ref[...] * rg_ref[...]), wo_ref[...])
    gate = _sigmoid(jnp.dot(h, wg_ref[...], preferred_element_type=F32))
    o_ref[0] = (gate * y_b).astype(BF16)


def _retention_tables():
    H, C = RET_HEADS, RET_CHUNK
    gamma = 1.0 - jnp.power(2.0, -5.0 - jnp.arange(H, dtype=F32))
    lg = jnp.log(gamma)
    idx = jnp.arange(C, dtype=F32)
    rel = idx[:, None] - idx[None, :]
    dmask = jnp.where(rel[None] >= 0, jnp.exp(jnp.maximum(rel, 0.0)[None] * lg[:, None, None]), 0.0)
    xi = jnp.exp((idx + 1.0)[None] * lg[:, None])
    zeta = jnp.exp((C - 1.0 - idx)[None] * lg[:, None])
    g_chunk = jnp.exp(C * lg)
    bc = lambda t: jnp.broadcast_to(t[:, :, None], (H, C, RET_HEAD_DIM)).astype(F32)
    g_rows = jnp.broadcast_to(g_chunk[:, None, None], (H, 1, RET_HEAD_DIM)).astype(F32)
    return dmask.astype(F32), bc(xi), bc(zeta), g_rows


def _branch_b(x, g, sh, sc, cosf, sinf, w, wg, tables, rg, wo):
    B, S, D = x.shape
    dr = wo.shape[0]
    tm = min(ROW_TILE, S)
    dmask, xi, zeta, g_chunk = tables
    vec = pl.BlockSpec((1, 1, D), lambda b, j: (b, 0, 0))
    rope_spec = pl.BlockSpec((1, tm, RET_HEAD_DIM), lambda b, j: (b, j, 0))
    return pl.pallas_call(
        _branch_b_kernel,
        grid=(B, S // tm),
        in_specs=[pl.BlockSpec((1, tm, D), lambda b, j: (b, j, 0)),
                  _const_spec((1, D)), vec, vec, rope_spec, rope_spec,
                  _const_spec(w.shape), _const_spec(wg.shape), _const_spec(dmask.shape),
                  _const_spec(xi.shape), _const_spec(zeta.shape), _const_spec(g_chunk.shape),
                  _const_spec(rg.shape), _const_spec(wo.shape)],
        out_specs=pl.BlockSpec((1, tm, D), lambda b, j: (b, j, 0)),
        out_shape=jax.ShapeDtypeStruct((B, S, D), BF16),
        scratch_shapes=[pltpu.VMEM((tm, dr), BF16), pltpu.VMEM((tm, dr), F32),
                        pltpu.VMEM((tm, dr), BF16), pltpu.VMEM((tm, dr), F32),
                        pltpu.VMEM((RET_HEADS, RET_HEAD_DIM, RET_HEAD_DIM), F32)],
        compiler_params=_params(2),
        name="branch_b",
    )(x, g, sh, sc, cosf, sinf, w, wg, dmask, xi, zeta, g_chunk, rg, wo)


def _out_router_kernel(ma_ref, mb_ref, mc_ref, x_ref, g1_ref, wo_ref, n2_ref, sh_ref, sc_ref,
                       rw_ref, rb_ref, x1_ref, h2_ref, idx_ref, prob_ref, rank_ref, cnt_ref,
                       carry_ref):
    tm = x_ref.shape[1]
    first = jnp.logical_and(pl.program_id(0) == 0, pl.program_id(1) == 0)

    @pl.when(first)
    def _():
        carry_ref[...] = jnp.zeros(carry_ref.shape, F32)

    m = ma_ref[0].astype(F32) + mb_ref[0].astype(F32) + mc_ref[0].astype(F32)
    x1 = x_ref[0] + g1_ref[0] * _bdot(m, wo_ref[...])
    x1_ref[0] = x1
    h2 = _norm_mod(x1, n2_ref[...], sh_ref[0], sc_ref[0])
    h2_ref[0] = h2

    logits = jnp.dot(h2, rw_ref[...], preferred_element_type=F32,
                     precision=lax.Precision.HIGHEST) + rb_ref[...]
    lane = lax.broadcasted_iota(jnp.int32, logits.shape, 1)
    lane_f = lane.astype(F32)
    neg = jnp.float32(-jnp.inf)
    work = jnp.where(lane < N_EXPERTS, logits, neg)
    idx_out = jnp.zeros(logits.shape, jnp.int32)
    val_out = jnp.full(logits.shape, neg, F32)
    onehot_all = jnp.zeros(logits.shape, F32)
    sels = []
    for kk in range(TOP_K):
        mx = jnp.max(work, axis=-1, keepdims=True)
        first_idx = jnp.min(jnp.where(work == mx, lane_f, float(LANES)), axis=-1, keepdims=True)
        sel = lane_f == first_idx
        sels.append(sel)
        idx_out = jnp.where(lane == kk, first_idx.astype(jnp.int32), idx_out)
        val_out = jnp.where(lane == kk, mx, val_out)
        onehot_all = onehot_all + sel.astype(F32)
        work = jnp.where(sel, neg, work)

    ex = jnp.exp(val_out - val_out[:, 0:1])
    prob_ref[0] = ex / jnp.sum(ex, axis=-1, keepdims=True)
    idx_ref[0] = idx_out

    row = lax.broadcasted_iota(jnp.int32, (tm, tm), 0)
    col = lax.broadcasted_iota(jnp.int32, (tm, tm), 1)
    lower = (col < row).astype(BF16)
    before = jnp.dot(lower, onehot_all.astype(BF16), preferred_element_type=F32) + carry_ref[...]
    rank_out = jnp.zeros(logits.shape, jnp.int32)
    for kk in range(TOP_K):
        rk = jnp.sum(jnp.where(sels[kk], before, 0.0), axis=-1, keepdims=True)
        rank_out = jnp.where(lane == kk, rk.astype(jnp.int32), rank_out)
    rank_ref[0] = rank_out
    total = carry_ref[...] + jnp.sum(onehot_all, axis=0, keepdims=True)
    carry_ref[...] = total
    cnt_ref[...] = total


def _out_router(ma, mb, mc, x, g1, wo, n2, sh2, sc2, rw, rb):
    B, S, D = x.shape
    tm = min(ROW_TILE, S)
    tile = pl.BlockSpec((1, tm, D), lambda b, j: (b, j, 0))
    vec = pl.BlockSpec((1, 1, D), lambda b, j: (b, 0, 0))
    lanes = pl.BlockSpec((1, tm, LANES), lambda b, j: (b, j, 0))
    return pl.pallas_call(
        _out_router_kernel,
        grid=(B, S // tm),
        in_specs=[tile, tile, tile, tile, vec, _const_spec(wo.shape), _const_spec((1, D)),
                  vec, vec, _const_spec(rw.shape), _const_spec(rb.shape)],
        out_specs=[tile, tile, lanes, lanes, lanes, pl.BlockSpec((1, LANES), lambda b, j: (0, 0))],
        out_shape=[jax.ShapeDtypeStruct((B, S, D), F32), jax.ShapeDtypeStruct((B, S, D), F32),
                   jax.ShapeDtypeStruct((B, S, LANES), jnp.int32),
                   jax.ShapeDtypeStruct((B, S, LANES), F32),
                   jax.ShapeDtypeStruct((B, S, LANES), jnp.int32),
                   jax.ShapeDtypeStruct((1, LANES), F32)],
        scratch_shapes=[pltpu.VMEM((1, LANES), F32)],
        compiler_params=_params(2),
        name="out_router",
    )(ma, mb, mc, x, g1, wo, n2, sh2, sc2, rw, rb)


def _start_row_gather(src_hbm, idx_ref, dst_ref, sem):
    for r in range(dst_ref.shape[0]):
        pltpu.make_async_copy(src_hbm.at[pl.ds(idx_ref[0, 0, r], 1)],
                              dst_ref.at[pl.ds(r, 1)], sem).start()


def _wait_row_gather(src_hbm, dst_ref, sem):
    pltpu.make_async_copy(src_hbm.at[pl.ds(0, dst_ref.shape[0])], dst_ref, sem).wait()


def _expert_kernel(be_ref, cur_idx_ref, nxt_idx_ref, h_hbm, w1_ref, b1_ref, w2_ref, b2_ref, o_ref,
                   xbuf0_ref, xbuf1_ref, w1b_ref, w2b_ref, sem_ref):
    i = pl.program_id(0)
    nb = pl.num_programs(0)
    f = w2_ref.shape[1]
    bufs = ((xbuf0_ref, sem_ref.at[0]), (xbuf1_ref, sem_ref.at[1]))

    @pl.when(i == 0)
    def _():
        _start_row_gather(h_hbm, cur_idx_ref, *bufs[0])

    @pl.when(jnp.logical_or(i == 0, be_ref[i] != be_ref[jnp.maximum(i - 1, 0)]))
    def _():
        w1b_ref[...] = w1_ref[0].astype(BF16)
        w2b_ref[...] = w2_ref[0].astype(BF16)

    def step(cur, nxt):
        _wait_row_gather(h_hbm, *cur)
        _start_row_gather(h_hbm, nxt_idx_ref, *nxt)
        xb = cur[0][...].astype(BF16)
        gu = jnp.dot(xb, w1b_ref[...], preferred_element_type=F32) + b1_ref[0]
        glu = jnp.minimum(gu[:, :f], SWIGLU_LIMIT)
        lin = jnp.clip(gu[:, f:], -SWIGLU_LIMIT, SWIGLU_LIMIT)
        act = glu * _sigmoid(SWIGLU_ALPHA * glu) * (lin + 1.0)
        o_ref[...] = _bdot(act, w2b_ref[...]) + b2_ref[0]

    for parity in range(2):
        @pl.when(lax.rem(i, 2) == parity)
        def _(parity=parity):
            step(bufs[parity], bufs[1 - parity])

        @pl.when(jnp.logical_and(i == nb - 1, lax.rem(i, 2) == parity))
        def _(parity=parity):
            _wait_row_gather(h_hbm, *bufs[1 - parity])


def _experts(h2, tok_of_slot, blk_expert, w1, b1, w2, b2):
    N, D = h2.shape
    E, _, F2 = w1.shape
    nb = blk_expert.shape[0]
    blk = EXPERT_BLOCK
    idx3 = tok_of_slot.reshape(nb, 1, blk)
    smem_blk = lambda fn: pl.BlockSpec((1, 1, blk), fn, memory_space=pltpu.SMEM)
    grid_spec = pltpu.PrefetchScalarGridSpec(
        num_scalar_prefetch=1,
        grid=(nb,),
        in_specs=[smem_blk(lambda i, be: (i, 0, 0)),
                  smem_blk(lambda i, be: (jnp.minimum(i + 1, nb - 1), 0, 0)),
                  pl.BlockSpec(memory_space=pl.ANY),
                  pl.BlockSpec((1, D, F2), lambda i, be: (be[i], 0, 0)),
                  pl.BlockSpec((1, 1, F2), lambda i, be: (be[i], 0, 0)),
                  pl.BlockSpec((1, F2 // 2, D), lambda i, be: (be[i], 0, 0)),
                  pl.BlockSpec((1, 1, D), lambda i, be: (be[i], 0, 0))],
        out_specs=pl.BlockSpec((blk, D), lambda i, be: (i, 0)),
        scratch_shapes=[pltpu.VMEM((blk, D), F32), pltpu.VMEM((blk, D), F32),
                        pltpu.VMEM((D, F2), BF16), pltpu.VMEM((F2 // 2, D), BF16),
                        pltpu.SemaphoreType.DMA((2,))])
    return pl.pallas_call(
        _expert_kernel,
        grid_spec=grid_spec,
        out_shape=jax.ShapeDtypeStruct((nb * blk, D), F32),
        compiler_params=_params(1),
        name="experts",
    )(blk_expert, idx3, idx3, h2, w1, b1.reshape(E, 1, F2), w2, b2.reshape(E, 1, D))


def _combine_kernel(cur_idx_ref, nxt_idx_ref, y_hbm, x1_ref, g2_ref, prob_ref, fg_ref, o_ref,
                    ybuf0_ref, ybuf1_ref, sem_ref, *, final_norm):
    i = pl.program_id(0)
    nt = pl.num_programs(0)
    tm = x1_ref.shape[0]
    bufs = ((ybuf0_ref, sem_ref.at[0]), (ybuf1_ref, sem_ref.at[1]))

    @pl.when(i == 0)
    def _():
        _start_row_gather(y_hbm, cur_idx_ref, *bufs[0])

    def step(cur, nxt):
        _wait_row_gather(y_hbm, *cur)
        _start_row_gather(y_hbm, nxt_idx_ref, *nxt)
        prob = prob_ref[...]
        acc = jnp.zeros(x1_ref.shape, F32)
        for kk in range(TOP_K):
            acc = acc + prob[:, kk:kk + 1] * cur[0][kk * tm:(kk + 1) * tm, :]
        x2 = x1_ref[...] + g2_ref[0] * acc
        if final_norm:
            x2 = x2 * lax.rsqrt(jnp.mean(x2 * x2, axis=-1, keepdims=True) + EPS) * fg_ref[...]
        o_ref[...] = x2

    for parity in range(2):
        @pl.when(lax.rem(i, 2) == parity)
        def _(parity=parity):
            step(bufs[parity], bufs[1 - parity])

        @pl.when(jnp.logical_and(i == nt - 1, lax.rem(i, 2) == parity))
        def _(parity=parity):
            _wait_row_gather(y_hbm, *bufs[1 - parity])


def _combine(ybuf, dest, x1, g2, prob, final_g, seq_len, final_norm):
    N, D = x1.shape
    tm = min(COMBINE_TILE, seq_len)
    nt = N // tm
    per_seq = seq_len // tm
    idx3 = dest.reshape(nt, tm, TOP_K).transpose(0, 2, 1).reshape(nt, 1, TOP_K * tm)
    smem_blk = lambda fn: pl.BlockSpec((1, 1, TOP_K * tm), fn, memory_space=pltpu.SMEM)
    return pl.pallas_call(
        functools.partial(_combine_kernel, final_norm=final_norm),
        grid=(nt,),
        in_specs=[smem_blk(lambda i: (i, 0, 0)),
                  smem_blk(lambda i: (jnp.minimum(i + 1, nt - 1), 0, 0)),
                  pl.BlockSpec(memory_space=pl.ANY),
                  pl.BlockSpec((tm, D), lambda i: (i, 0)),
                  pl.BlockSpec((1, 1, D), lambda i: (i // per_seq, 0, 0)),
                  pl.BlockSpec((tm, LANES), lambda i: (i, 0)),
                  pl.BlockSpec((1, D), lambda i: (0, 0))],
        out_specs=pl.BlockSpec((tm, D), lambda i: (i, 0)),
        out_shape=jax.ShapeDtypeStruct((N, D), F32),
        scratch_shapes=[pltpu.VMEM((TOP_K * tm, D), F32), pltpu.VMEM((TOP_K * tm, D), F32),
                        pltpu.SemaphoreType.DMA((2,))],
        compiler_params=_params(1),
        name="combine",
    )(idx3, idx3, ybuf, x1, g2, prob, final_g)


def _routing_tables(counts, idx, rank, n_tokens):
    blk = EXPERT_BLOCK
    A = n_tokens * TOP_K
    nb = (A + N_EXPERTS * (blk - 1)) // blk
    counts = counts.astype(jnp.int32)
    padded = ((counts + blk - 1) // blk) * blk
    pends = jnp.cumsum(padded)
    pstarts = pends - padded
    block_start = jnp.arange(nb, dtype=jnp.int32) * blk
    blk_expert = jnp.minimum(jnp.sum(block_start[:, None] >= pends[None, :], axis=1),
                             N_EXPERTS - 1).astype(jnp.int32)
    dest = pstarts[idx] + rank
    tok = jnp.repeat(jnp.arange(n_tokens, dtype=jnp.int32), TOP_K)
    tok_of_slot = jnp.zeros((nb * blk,), jnp.int32).at[dest.reshape(A)].set(
        tok, unique_indices=True, mode='promise_in_bounds')
    return dest, tok_of_slot, blk_expert


def kernel(x, c, positions, mod_w, mod_b, norm1_g, w_in, conv_a_w, w_a_out, ret_norm_g, w_b_out, glu_b, conv_c_w, conv_c_b, ln_c_g, ln_c_b, w_c_out, w_o, norm2_g, router_w, router_b, exp_w1, exp_b1, exp_w2, exp_b2, final_g):
    B, S, D = x.shape
    L = mod_w.shape[0]
    N = B * S
    da = w_a_out.shape[1]
    dr = w_b_out.shape[1]
    dc = w_c_out.shape[1]
    s1 = 3 * da
    s2 = s1 + 4 * dr
    s3 = s2 + 2 * dc

    mod = _modulation(c, mod_w, mod_b)
    cosf, sinf = _rope_tables(positions)
    tables = _retention_tables()
    row = lambda v: v.reshape(1, -1)
    rw_pad = jnp.zeros((L, D, LANES), F32).at[:, :, :N_EXPERTS].set(router_w)
    rb_pad = jnp.zeros((L, 1, LANES), F32).at[:, 0, :N_EXPERTS].set(router_b)

    for l in range(L):
        sh1, sc1, g1, sh2, sc2, g2 = [mod[l, :, i * D:(i + 1) * D].reshape(B, 1, D) for i in range(6)]
        wl = w_in[l].astype(BF16)
        wg = wl[:, s3:]
        n1 = row(norm1_g[l])
        ma = _branch_a(x, n1, sh1, sc1, wl[:, :s1], wg[:, :D], conv_a_w[l],
                       w_a_out[l].astype(BF16))
        mb = _branch_b(x, n1, sh1, sc1, cosf, sinf, wl[:, s1:s2], wg[:, D:2 * D], tables,
                       row(ret_norm_g[l]), w_b_out[l].astype(BF16))
        mc = _branch_c(x, n1, sh1, sc1, wl[:, s2:s3], wg[:, 2 * D:], row(glu_b[l]), conv_c_w[l],
                       row(conv_c_b[l]), row(ln_c_g[l]), row(ln_c_b[l]), w_c_out[l].astype(BF16))
        x1, h2, idx, prob, rank, counts = _out_router(
            ma, mb, mc, x, g1, w_o[l].astype(BF16), row(norm2_g[l]), sh2, sc2, rw_pad[l], rb_pad[l])
        idx = idx.reshape(N, LANES)[:, :TOP_K]
        rank = rank.reshape(N, LANES)[:, :TOP_K]
        dest, tok_of_slot, blk_expert = _routing_tables(counts[0, :N_EXPERTS], idx, rank, N)
        ybuf = _experts(h2.reshape(N, D), tok_of_slot, blk_expert,
                        exp_w1[l], exp_b1[l], exp_w2[l], exp_b2[l])
        x = _combine(ybuf, dest, x1.reshape(N, D), g2, prob.reshape(N, LANES), row(final_g),
                     S, final_norm=(l == L - 1)).reshape(B, S, D)
    return x
```

```python
import functools

import jax
import jax.numpy as jnp
from jax import lax
from jax.experimental import pallas as pl
from jax.experimental.pallas import tpu as pltpu

F32 = jnp.float32
BF16 = jnp.bfloat16

EPS = 1e-6
SHORT_CONV_W = 3
RET_HEADS = 8
RET_HEAD_DIM = 128
RET_CHUNK = 128
ROPE_BASE = 10000.0
CONF_KERNEL = 31
N_EXPERTS = 32
TOP_K = 4
SWIGLU_LIMIT = 7.0
SWIGLU_ALPHA = 1.702

LANES = 128
SUBLANES = 8
VMEM_LIMIT = 56 * 1024 * 1024

ROW_TILE = 512
EXPERT_BLOCK = 256
COMBINE_TILE = 256
CONV_ROWS = 32
CONF_HALO = 32


def _params(n_grid):
    return pltpu.CompilerParams(
        dimension_semantics=("arbitrary",) * n_grid, vmem_limit_bytes=VMEM_LIMIT)


def _const_spec(shape):
    nd = len(shape)
    return pl.BlockSpec(shape, lambda *_: (0,) * nd, pipeline_mode=pl.Buffered(1))


def _sigmoid(x):
    return 1.0 / (1.0 + jnp.exp(-x))


def _silu(x):
    return x * _sigmoid(x)


def _norm_mod(x, g, sh, sc):
    y = x * lax.rsqrt(jnp.mean(x * x, axis=-1, keepdims=True) + EPS)
    return (y * g) * (1.0 + sc) + sh


def _bdot(a, b):
    return jnp.dot(a.astype(BF16), b.astype(BF16), preferred_element_type=F32)


def _mod_kernel(c_ref, w_ref, b_ref, o_ref):
    cond = _silu(c_ref[...])
    o_ref[0] = jnp.dot(cond, w_ref[0], preferred_element_type=F32,
                       precision=lax.Precision.HIGHEST) + b_ref[0]


def _modulation(c, mod_w, mod_b):
    L, D, M = mod_w.shape
    B = c.shape[0]
    tn = 1024
    return pl.pallas_call(
        _mod_kernel,
        grid=(L, M // tn),
        in_specs=[pl.BlockSpec((B, D), lambda l, j: (0, 0)),
                  pl.BlockSpec((1, D, tn), lambda l, j: (l, 0, j)),
                  pl.BlockSpec((1, 1, tn), lambda l, j: (l, 0, j))],
        out_specs=pl.BlockSpec((1, B, tn), lambda l, j: (l, 0, j)),
        out_shape=jax.ShapeDtypeStruct((L, B, M), F32),
        compiler_params=_params(2),
        name="modulation",
    )(c, mod_w, mod_b.reshape(L, 1, M))


def _rope_kernel(pos_ref, invf_ref, cos_ref, sin_ref):
    ang = pos_ref[0].astype(F32) * invf_ref[...]
    lane = lax.broadcasted_iota(jnp.int32, ang.shape, 1)
    s = jnp.sin(ang)
    cos_ref[0] = jnp.cos(ang)
    sin_ref[0] = jnp.where(lane < RET_HEAD_DIM // 2, -s, s)


def _rope_tables(positions):
    B, S = positions.shape
    inv_freq = 1.0 / (ROPE_BASE ** (jnp.arange(0, RET_HEAD_DIM, 2, dtype=F32) / RET_HEAD_DIM))
    invf = jnp.concatenate([inv_freq, inv_freq]).reshape(1, RET_HEAD_DIM)
    ts = min(S, 1024)
    spec = pl.BlockSpec((1, ts, RET_HEAD_DIM), lambda b, j: (b, j, 0))
    return pl.pallas_call(
        _rope_kernel,
        grid=(B, S // ts),
        in_specs=[pl.BlockSpec((1, ts, 1), lambda b, j: (b, j, 0)),
                  pl.BlockSpec((1, RET_HEAD_DIM), lambda b, j: (0, 0))],
        out_specs=[spec, spec],
        out_shape=[jax.ShapeDtypeStruct((B, S, RET_HEAD_DIM), F32)] * 2,
        compiler_params=_params(2),
        name="rope_tables",
    )(positions.reshape(B, S, 1), invf)


def _branch_a_kernel(x_ref, g_ref, sh_ref, sc_ref, w_ref, wg_ref, cw_ref, wo_ref, o_ref, buf_ref):
    tm = x_ref.shape[1]
    da = wo_ref.shape[0]

    @pl.when(pl.program_id(1) == 0)
    def _():
        buf_ref[0:SUBLANES, :] = jnp.zeros((SUBLANES, da), F32)

    h = _norm_mod(x_ref[0], g_ref[...], sh_ref[0], sc_ref[0]).astype(BF16)
    p = jnp.dot(h, w_ref[...], preferred_element_type=F32)
    b_g, c_g, xa = p[:, :da], p[:, da:2 * da], p[:, 2 * da:]
    u = c_g * xa
    buf_ref[SUBLANES:SUBLANES + tm, :] = u
    conv = cw_ref[2:3, :] * u
    for j in range(SHORT_CONV_W - 1):
        off = SUBLANES - (SHORT_CONV_W - 1) + j
        conv = conv + cw_ref[j:j + 1, :] * buf_ref[off:off + tm, :]
    buf_ref[0:SUBLANES, :] = buf_ref[tm:tm + SUBLANES, :]
    y_a = _bdot(b_g * conv, wo_ref[...])
    gate = _sigmoid(jnp.dot(h, wg_ref[...], preferred_element_type=F32))
    o_ref[0] = (gate * y_a).astype(BF16)


def _branch_a(x, g, sh, sc, w, wg, cw, wo):
    B, S, D = x.shape
    da = wo.shape[0]
    tm = min(ROW_TILE, S)
    vec = pl.BlockSpec((1, 1, D), lambda b, j: (b, 0, 0))
    return pl.pallas_call(
        _branch_a_kernel,
        grid=(B, S // tm),
        in_specs=[pl.BlockSpec((1, tm, D), lambda b, j: (b, j, 0)),
                  _const_spec((1, D)), vec, vec,
                  _const_spec(w.shape), _const_spec(wg.shape), _const_spec(cw.shape),
                  _const_spec(wo.shape)],
        out_specs=pl.BlockSpec((1, tm, D), lambda b, j: (b, j, 0)),
        out_shape=jax.ShapeDtypeStruct((B, S, D), BF16),
        scratch_shapes=[pltpu.VMEM((tm + SUBLANES, da), F32)],
        compiler_params=_params(2),
        name="branch_a",
    )(x, g, sh, sc, w, wg, cw, wo)


def _branch_c_kernel(x_ref, g_ref, sh_ref, sc_ref, w_ref, wg_ref, glub_ref, cw_ref, cb_ref,
                     lng_ref, lnb_ref, wo_ref, o_ref, buf_ref, conv_ref):
    tm = x_ref.shape[1]
    dc = wo_ref.shape[0]

    @pl.when(pl.program_id(1) == 0)
    def _():
        buf_ref[0, 0:CONF_HALO, :] = jnp.zeros((CONF_HALO, dc), F32)

    h = _norm_mod(x_ref[0], g_ref[...], sh_ref[0], sc_ref[0]).astype(BF16)
    p = jnp.dot(h, w_ref[...], preferred_element_type=F32) + glub_ref[...]
    buf_ref[0, CONF_HALO:CONF_HALO + tm, :] = p[:, :dc] * _sigmoid(p[:, dc:])
    for b in range(1, SUBLANES):
        buf_ref[b, SUBLANES:, :] = buf_ref[0, SUBLANES - b:tm + CONF_HALO - b, :]

    base = CONF_HALO - (CONF_KERNEL - 1)

    def strip(s, carry):
        r0 = pl.multiple_of(s * CONV_ROWS, CONV_ROWS)
        acc = jnp.zeros((CONV_ROWS, dc), F32)
        for j in range(CONF_KERNEL):
            a = -(-(base + j) // SUBLANES)
            b = a * SUBLANES - (base + j)
            acc = acc + cw_ref[j:j + 1, :] * buf_ref[b, pl.ds(r0 + a * SUBLANES, CONV_ROWS), :]
        conv_ref[pl.ds(r0, CONV_ROWS), :] = acc
        return carry

    lax.fori_loop(0, tm // CONV_ROWS, strip, 0)
    buf_ref[0, 0:CONF_HALO, :] = buf_ref[0, tm:tm + CONF_HALO, :]

    u = conv_ref[...] + cb_ref[...]
    mu = jnp.mean(u, axis=-1, keepdims=True)
    d = u - mu
    var = jnp.mean(d * d, axis=-1, keepdims=True)
    y = d * lax.rsqrt(var + EPS) * lng_ref[...] + lnb_ref[...]
    y_c = _bdot(_silu(y), wo_ref[...])
    gate = _sigmoid(jnp.dot(h, wg_ref[...], preferred_element_type=F32))
    o_ref[0] = (gate * y_c).astype(BF16)


def _branch_c(x, g, sh, sc, w, wg, glub, cw, cb, lng, lnb, wo):
    B, S, D = x.shape
    dc = wo.shape[0]
    tm = min(ROW_TILE, S)
    vec = pl.BlockSpec((1, 1, D), lambda b, j: (b, 0, 0))
    return pl.pallas_call(
        _branch_c_kernel,
        grid=(B, S // tm),
        in_specs=[pl.BlockSpec((1, tm, D), lambda b, j: (b, j, 0)),
                  _const_spec((1, D)), vec, vec,
                  _const_spec(w.shape), _const_spec(wg.shape), _const_spec(glub.shape),
                  _const_spec(cw.shape), _const_spec(cb.shape), _const_spec(lng.shape),
                  _const_spec(lnb.shape), _const_spec(wo.shape)],
        out_specs=pl.BlockSpec((1, tm, D), lambda b, j: (b, j, 0)),
        out_shape=jax.ShapeDtypeStruct((B, S, D), BF16),
        scratch_shapes=[pltpu.VMEM((SUBLANES, tm + CONF_HALO, dc), F32), pltpu.VMEM((tm, dc), F32)],
        compiler_params=_params(2),
        name="branch_c",
    )(x, g, sh, sc, w, wg, glub, cw, cb, lng, lnb, wo)


def _branch_b_kernel(x_ref, g_ref, sh_ref, sc_ref, cos_ref, sin_ref, w_ref, wg_ref,
                     dmask_ref, xi_ref, zeta_ref, gch_ref, rg_ref, wo_ref, o_ref,
                     q_ref, k_ref, v_ref, r_ref, state_ref):
    tm = x_ref.shape[1]
    dr = wo_ref.shape[0]
    hd = RET_HEAD_DIM

    @pl.when(pl.program_id(1) == 0)
    def _():
        state_ref[...] = jnp.zeros(state_ref.shape, F32)

    h = _norm_mod(x_ref[0], g_ref[...], sh_ref[0], sc_ref[0]).astype(BF16)
    cos = cos_ref[0]
    sin = sin_ref[0]
    scale = hd ** -0.5

    def rope(t):
        return t * cos + pltpu.roll(t, hd // 2, axis=1) * sin

    q = jnp.dot(h, w_ref[:, 0:dr], preferred_element_type=F32) * scale
    k = jnp.dot(h, w_ref[:, dr:2 * dr], preferred_element_type=F32)
    for hh in range(RET_HEADS):
        sl = slice(hh * hd, (hh + 1) * hd)
        q_ref[:, sl] = rope(q[:, sl]).astype(BF16)
        k_ref[:, sl] = rope(k[:, sl])
    v_ref[...] = jnp.dot(h, w_ref[:, 2 * dr:3 * dr], preferred_element_type=F32).astype(BF16)

    def chunk(ci, carry):
        r0 = pl.multiple_of(ci * RET_CHUNK, RET_CHUNK)
        rows = pl.ds(r0, RET_CHUNK)
        for hh in range(RET_HEADS):
            sl = slice(hh * hd, (hh + 1) * hd)
            qh = q_ref[rows, sl]
            kh = k_ref[rows, sl]
            vh = v_ref[rows, sl]
            st = state_ref[hh]
            scores = lax.dot_general(qh, kh.astype(BF16), (((1,), (1,)), ((), ())),
                                     preferred_element_type=F32) * dmask_ref[hh]
            inner = _bdot(scores, vh)
            cross = _bdot(qh, st) * xi_ref[hh]
            kz = (kh * zeta_ref[hh]).astype(BF16)
            state_ref[hh] = st * gch_ref[hh] + lax.dot_general(
                kz, vh, (((0,), (0,)), ((), ())), preferred_element_type=F32)
            o = inner + cross
            mu = jnp.mean(o, axis=-1, keepdims=True)
            d = o - mu
            var = jnp.mean(d * d, axis=-1, keepdims=True)
            r_ref[rows, sl] = d * lax.rsqrt(var + EPS)
        return carry

    lax.fori_loop(0, tm // RET_CHUNK, chunk, 0)

    og = jnp.dot(h, w_ref[:, 3 * dr:4 * dr], preferred_element_type=F32)
    y_b = _bdot(_silu(og) * (r_ref[...] * rg_ref[...]), wo_ref[...])
    gate = _sigmoid(jnp.dot(h, wg_ref[...], preferred_element_type=F32))
    o_ref[0] = (gate * y_b).astype(BF16)


def _retention_tables():
    H, C = RET_HEADS, RET_CHUNK
    gamma = 1.0 - jnp.power(2.0, -5.0 - jnp.arange(H, dtype=F32))
    lg = jnp.log(gamma)
    idx = jnp.arange(C, dtype=F32)
    rel = idx[:, None] - idx[None, :]
    dmask = jnp.where(rel[None] >= 0, jnp.exp(jnp.maximum(rel, 0.0)[None] * lg[:, None, None]), 0.0)
    xi = jnp.exp((idx + 1.0)[None] * lg[:, None])
    zeta = jnp.exp((C - 1.0 - idx)[None] * lg[:, None])
    g_chunk = jnp.exp(C * lg)
    bc = lambda t: jnp.broadcast_to(t[:, :, None], (H, C, RET_HEAD_DIM)).astype(F32)
    g_rows = jnp.broadcast_to(g_chunk[:, None, None], (H, 1, RET_HEAD_DIM)).astype(F32)
    return dmask.astype(F32), bc(xi), bc(zeta), g_rows


def _branch_b(x, g, sh, sc, cosf, sinf, w, wg, tables, rg, wo):
    B, S, D = x.shape
    dr = wo.shape[0]
    tm = min(ROW_TILE, S)
    dmask, xi, zeta, g_chunk = tables
    vec = pl.BlockSpec((1, 1, D), lambda b, j: (b, 0, 0))
    rope_spec = pl.BlockSpec((1, tm, RET_HEAD_DIM), lambda b, j: (b, j, 0))
    return pl.pallas_call(
        _branch_b_kernel,
        grid=(B, S // tm),
        in_specs=[pl.BlockSpec((1, tm, D), lambda b, j: (b, j, 0)),
                  _const_spec((1, D)), vec, vec, rope_spec, rope_spec,
                  _const_spec(w.shape), _const_spec(wg.shape), _const_spec(dmask.shape),
                  _const_spec(xi.shape), _const_spec(zeta.shape), _const_spec(g_chunk.shape),
                  _const_spec(rg.shape), _const_spec(wo.shape)],
        out_specs=pl.BlockSpec((1, tm, D), lambda b, j: (b, j, 0)),
        out_shape=jax.ShapeDtypeStruct((B, S, D), BF16),
        scratch_shapes=[pltpu.VMEM((tm, dr), BF16), pltpu.VMEM((tm, dr), F32),
                        pltpu.VMEM((tm, dr), BF16), pltpu.VMEM((tm, dr), F32),
                        pltpu.VMEM((RET_HEADS, RET_HEAD_DIM, RET_HEAD_DIM), F32)],
        compiler_params=_params(2),
        name="branch_b",
    )(x, g, sh, sc, cosf, sinf, w, wg, dmask, xi, zeta, g_chunk, rg, wo)


def _out_router_kernel(ma_ref, mb_ref, mc_ref, x_ref, g1_ref, wo_ref, n2_ref, sh_ref, sc_ref,
                       rw_ref, rb_ref, x1_ref, h2_ref, idx_ref, prob_ref, rank_ref, cnt_ref,
                       carry_ref):
    tm = x_ref.shape[1]
    first = jnp.logical_and(pl.program_id(0) == 0, pl.program_id(1) == 0)

    @pl.when(first)
    def _():
        carry_ref[...] = jnp.zeros(carry_ref.shape, F32)

    m = ma_ref[0].astype(F32) + mb_ref[0].astype(F32) + mc_ref[0].astype(F32)
    x1 = x_ref[0] + g1_ref[0] * _bdot(m, wo_ref[...])
    x1_ref[0] = x1
    h2 = _norm_mod(x1, n2_ref[...], sh_ref[0], sc_ref[0])
    h2_ref[0] = h2

    logits = jnp.dot(h2, rw_ref[...], preferred_element_type=F32,
                     precision=lax.Precision.HIGHEST) + rb_ref[...]
    lane = lax.broadcasted_iota(jnp.int32, logits.shape, 1)
    lane_f = lane.astype(F32)
    neg = jnp.float32(-jnp.inf)
    work = jnp.where(lane < N_EXPERTS, logits, neg)
    idx_out = jnp.zeros(logits.shape, jnp.int32)
    val_out = jnp.full(logits.shape, neg, F32)
    onehot_all = jnp.zeros(logits.shape, F32)
    sels = []
    for kk in range(TOP_K):
        mx = jnp.max(work, axis=-1, keepdims=True)
        first_idx = jnp.min(jnp.where(work == mx, lane_f, float(LANES)), axis=-1, keepdims=True)
        sel = lane_f == first_idx
        sels.append(sel)
        idx_out = jnp.where(lane == kk, first_idx.astype(jnp.int32), idx_out)
        val_out = jnp.where(lane == kk, mx, val_out)
        onehot_all = onehot_all + sel.astype(F32)
        work = jnp.where(sel, neg, work)

    ex = jnp.exp(val_out - val_out[:, 0:1])
    prob_ref[0] = ex / jnp.sum(ex, axis=-1, keepdims=True)
    idx_ref[0] = idx_out

    row = lax.broadcasted_iota(jnp.int32, (tm, tm), 0)
    col = lax.broadcasted_iota(jnp.int32, (tm, tm), 1)
    lower = (col < row).astype(BF16)
    before = jnp.dot(lower, onehot_all.astype(BF16), preferred_element_type=F32) + carry_ref[...]
    rank_out = jnp.zeros(logits.shape, jnp.int32)
    for kk in range(TOP_K):
        rk = jnp.sum(jnp.where(sels[kk], before, 0.0), axis=-1, keepdims=True)
        rank_out = jnp.where(lane == kk, rk.astype(jnp.int32), rank_out)
    rank_ref[0] = rank_out
    total = carry_ref[...] + jnp.sum(onehot_all, axis=0, keepdims=True)
    carry_ref[...] = total
    cnt_ref[...] = total


def _out_router(ma, mb, mc, x, g1, wo, n2, sh2, sc2, rw, rb):
    B, S, D = x.shape
    tm = min(ROW_TILE, S)
    tile = pl.BlockSpec((1, tm, D), lambda b, j: (b, j, 0))
    vec = pl.BlockSpec((1, 1, D), lambda b, j: (b, 0, 0))
    lanes = pl.BlockSpec((1, tm, LANES), lambda b, j: (b, j, 0))
    return pl.pallas_call(
        _out_router_kernel,
        grid=(B, S // tm),
        in_specs=[tile, tile, tile, tile, vec, _const_spec(wo.shape), _const_spec((1, D)),
                  vec, vec, _const_spec(rw.shape), _const_spec(rb.shape)],
        out_specs=[tile, tile, lanes, lanes, lanes, pl.BlockSpec((1, LANES), lambda b, j: (0, 0))],
        out_shape=[jax.ShapeDtypeStruct((B, S, D), F32), jax.ShapeDtypeStruct((B, S, D), F32),
                   jax.ShapeDtypeStruct((B, S, LANES), jnp.int32),
                   jax.ShapeDtypeStruct((B, S, LANES), F32),
                   jax.ShapeDtypeStruct((B, S, LANES), jnp.int32),
                   jax.ShapeDtypeStruct((1, LANES), F32)],
        scratch_shapes=[pltpu.VMEM((1, LANES), F32)],
        compiler_params=_params(2),
        name="out_router",
    )(ma, mb, mc, x, g1, wo, n2, sh2, sc2, rw, rb)


def _start_row_gather(src_hbm, idx_ref, dst_ref, sem):
    for r in range(dst_ref.shape[0]):
        pltpu.make_async_copy(src_hbm.at[pl.ds(idx_ref[0, 0, r], 1)],
                              dst_ref.at[pl.ds(r, 1)], sem).start()


def _wait_row_gather(src_hbm, dst_ref, sem):
    pltpu.make_async_copy(src_hbm.at[pl.ds(0, dst_ref.shape[0])], dst_ref, sem).wait()


def _expert_kernel(be_ref, idx0_ref, idx1_ref, idx2_ref, h_hbm, w1_ref, b1_ref, w2_ref, b2_ref,
                   o_ref, xbuf0_ref, xbuf1_ref, xbuf2_ref, w1b_ref, w2b_ref, sem_ref):
    i = pl.program_id(0)
    nb = pl.num_programs(0)
    f = w2_ref.shape[2]
    bufs = ((xbuf0_ref, sem_ref.at[0]), (xbuf1_ref, sem_ref.at[1]), (xbuf2_ref, sem_ref.at[2]))
    depth = len(bufs)

    @pl.when(i == 0)
    def _():
        _start_row_gather(h_hbm, idx0_ref, *bufs[0])
        _start_row_gather(h_hbm, idx1_ref, *bufs[1])

    @pl.when(jnp.logical_or(i == 0, be_ref[i] != be_ref[jnp.maximum(i - 1, 0)]))
    def _():
        w1b_ref[...] = w1_ref[0, 0].astype(BF16)
        w2b_ref[...] = w2_ref[0, 0].astype(BF16)

    def step(cur, ahead):
        _wait_row_gather(h_hbm, *cur)
        _start_row_gather(h_hbm, idx2_ref, *ahead)
        xb = cur[0][...].astype(BF16)
        gu = jnp.dot(xb, w1b_ref[...], preferred_element_type=F32) + b1_ref[0, 0]
        glu = jnp.minimum(gu[:, :f], SWIGLU_LIMIT)
        lin = jnp.clip(gu[:, f:], -SWIGLU_LIMIT, SWIGLU_LIMIT)
        act = glu * _sigmoid(SWIGLU_ALPHA * glu) * (lin + 1.0)
        o_ref[...] = _bdot(act, w2b_ref[...]) + b2_ref[0, 0]

    for p in range(depth):
        @pl.when(lax.rem(i, depth) == p)
        def _(p=p):
            step(bufs[p], bufs[(p + 2) % depth])

        @pl.when(jnp.logical_and(i == nb - 1, lax.rem(i, depth) == p))
        def _(p=p):
            _wait_row_gather(h_hbm, *bufs[(p + 1) % depth])
            _wait_row_gather(h_hbm, *bufs[(p + 2) % depth])


def _experts(h2, tok_of_slot, blk_expert, w1, b1, w2, b2, layer):
    N, D = h2.shape
    L, E, _, F2 = w1.shape
    nb = blk_expert.shape[0]
    blk = EXPERT_BLOCK
    idx3 = tok_of_slot.reshape(nb, 1, blk)
    smem_blk = lambda k: pl.BlockSpec((1, 1, blk), lambda i, be: (jnp.minimum(i + k, nb - 1), 0, 0),
                                      memory_space=pltpu.SMEM)
    per_expert = lambda shape: pl.BlockSpec((1, 1) + shape, lambda i, be: (layer, be[i], 0, 0))
    grid_spec = pltpu.PrefetchScalarGridSpec(
        num_scalar_prefetch=1,
        grid=(nb,),
        in_specs=[smem_blk(0), smem_blk(1), smem_blk(2),
                  pl.BlockSpec(memory_space=pl.ANY),
                  per_expert((D, F2)), per_expert((1, F2)), per_expert((F2 // 2, D)),
                  per_expert((1, D))],
        out_specs=pl.BlockSpec((blk, D), lambda i, be: (i, 0)),
        scratch_shapes=[pltpu.VMEM((blk, D), F32), pltpu.VMEM((blk, D), F32),
                        pltpu.VMEM((blk, D), F32),
                        pltpu.VMEM((D, F2), BF16), pltpu.VMEM((F2 // 2, D), BF16),
                        pltpu.SemaphoreType.DMA((3,))])
    return pl.pallas_call(
        _expert_kernel,
        grid_spec=grid_spec,
        out_shape=jax.ShapeDtypeStruct((nb * blk, D), F32),
        compiler_params=_params(1),
        name="experts",
    )(blk_expert, idx3, idx3, idx3, h2, w1, b1.reshape(L, E, 1, F2), w2, b2.reshape(L, E, 1, D))


def _combine_kernel(cur_idx_ref, nxt_idx_ref, y_hbm, x1_ref, g2_ref, prob_ref, fg_ref, o_ref,
                    ybuf0_ref, ybuf1_ref, sem_ref, *, final_norm):
    i = pl.program_id(0)
    nt = pl.num_programs(0)
    tm = x1_ref.shape[0]
    bufs = ((ybuf0_ref, sem_ref.at[0]), (ybuf1_ref, sem_ref.at[1]))

    @pl.when(i == 0)
    def _():
        _start_row_gather(y_hbm, cur_idx_ref, *bufs[0])

    def step(cur, nxt):
        _wait_row_gather(y_hbm, *cur)
        _start_row_gather(y_hbm, nxt_idx_ref, *nxt)
        prob = prob_ref[...]
        acc = jnp.zeros(x1_ref.shape, F32)
        for kk in range(TOP_K):
            acc = acc + prob[:, kk:kk + 1] * cur[0][kk * tm:(kk + 1) * tm, :]
        x2 = x1_ref[...] + g2_ref[0] * acc
        if final_norm:
            x2 = x2 * lax.rsqrt(jnp.mean(x2 * x2, axis=-1, keepdims=True) + EPS) * fg_ref[...]
        o_ref[...] = x2

    for parity in range(2):
        @pl.when(lax.rem(i, 2) == parity)
        def _(parity=parity):
            step(bufs[parity], bufs[1 - parity])

        @pl.when(jnp.logical_and(i == nt - 1, lax.rem(i, 2) == parity))
        def _(parity=parity):
            _wait_row_gather(y_hbm, *bufs[1 - parity])


def _combine(ybuf, dest, x1, g2, prob, final_g, seq_len, final_norm):
    N, D = x1.shape
    tm = min(COMBINE_TILE, seq_len)
    nt = N // tm
    per_seq = seq_len // tm
    idx3 = dest.reshape(nt, tm, TOP_K).transpose(0, 2, 1).reshape(nt, 1, TOP_K * tm)
    smem_blk = lambda fn: pl.BlockSpec((1, 1, TOP_K * tm), fn, memory_space=pltpu.SMEM)
    return pl.pallas_call(
        functools.partial(_combine_kernel, final_norm=final_norm),
        grid=(nt,),
        in_specs=[smem_blk(lambda i: (i, 0, 0)),
                  smem_blk(lambda i: (jnp.minimum(i + 1, nt - 1), 0, 0)),
                  pl.BlockSpec(memory_space=pl.ANY),
                  pl.BlockSpec((tm, D), lambda i: (i, 0)),
                  pl.BlockSpec((1, 1, D), lambda i: (i // per_seq, 0, 0)),
                  pl.BlockSpec((tm, LANES), lambda i: (i, 0)),
                  pl.BlockSpec((1, D), lambda i: (0, 0))],
        out_specs=pl.BlockSpec((tm, D), lambda i: (i, 0)),
        out_shape=jax.ShapeDtypeStruct((N, D), F32),
        scratch_shapes=[pltpu.VMEM((TOP_K * tm, D), F32), pltpu.VMEM((TOP_K * tm, D), F32),
                        pltpu.SemaphoreType.DMA((2,))],
        compiler_params=_params(1),
        name="combine",
    )(idx3, idx3, ybuf, x1, g2, prob, final_g)


def _routing_tables(counts, idx, rank, n_tokens):
    blk = EXPERT_BLOCK
    A = n_tokens * TOP_K
    nb = (A + N_EXPERTS * (blk - 1)) // blk
    counts = counts.astype(jnp.int32)
    padded = ((counts + blk - 1) // blk) * blk
    pends = jnp.cumsum(padded)
    pstarts = pends - padded
    block_start = jnp.arange(nb, dtype=jnp.int32) * blk
    blk_expert = jnp.minimum(jnp.sum(block_start[:, None] >= pends[None, :], axis=1),
                             N_EXPERTS - 1).astype(jnp.int32)
    dest = pstarts[idx] + rank
    tok = jnp.repeat(jnp.arange(n_tokens, dtype=jnp.int32), TOP_K)
    n_pad = nb * blk - A
    pad_per = padded - counts
    padcum = jnp.cumsum(pad_per)
    q = jnp.arange(n_pad, dtype=jnp.int32)
    e_q = jnp.sum(q[:, None] >= padcum[None, :], axis=1).astype(jnp.int32)
    e_c = jnp.minimum(e_q, N_EXPERTS - 1)
    key_in = (pstarts + counts - (padcum - pad_per))[e_c] + q
    key_tail = pends[-1] - padcum[-1] + q
    pad_keys = jnp.where(e_q < N_EXPERTS, key_in, key_tail)
    keys = jnp.concatenate([dest.reshape(A), pad_keys])
    vals = jnp.concatenate([tok, jnp.zeros((n_pad,), jnp.int32)])
    _, tok_of_slot = lax.sort((keys, vals), num_keys=1)
    return dest, tok_of_slot, blk_expert


def kernel(x, c, positions, mod_w, mod_b, norm1_g, w_in, conv_a_w, w_a_out, ret_norm_g, w_b_out, glu_b, conv_c_w, conv_c_b, ln_c_g, ln_c_b, w_c_out, w_o, norm2_g, router_w, router_b, exp_w1, exp_b1, exp_w2, exp_b2, final_g):
    B, S, D = x.shape
    L = mod_w.shape[0]
    N = B * S
    da = w_a_out.shape[1]
    dr = w_b_out.shape[1]
    dc = w_c_out.shape[1]
    s1 = 3 * da
    s2 = s1 + 4 * dr
    s3 = s2 + 2 * dc

    mod = _modulation(c, mod_w, mod_b)
    cosf, sinf = _rope_tables(positions)
    tables = _retention_tables()
    row = lambda v: v.reshape(1, -1)
    rw_pad = jnp.zeros((L, D, LANES), F32).at[:, :, :N_EXPERTS].set(router_w)
    rb_pad = jnp.zeros((L, 1, LANES), F32).at[:, 0, :N_EXPERTS].set(router_b)

    for l in range(L):
        sh1, sc1, g1, sh2, sc2, g2 = [mod[l, :, i * D:(i + 1) * D].reshape(B, 1, D) for i in range(6)]
        wl = w_in[l].astype(BF16)
        wg = wl[:, s3:]
        n1 = row(norm1_g[l])
        ma = _branch_a(x, n1, sh1, sc1, wl[:, :s1], wg[:, :D], conv_a_w[l],
                       w_a_out[l].astype(BF16))
        mb = _branch_b(x, n1, sh1, sc1, cosf, sinf, wl[:, s1:s2], wg[:, D:2 * D], tables,
                       row(ret_norm_g[l]), w_b_out[l].astype(BF16))
        mc = _branch_c(x, n1, sh1, sc1, wl[:, s2:s3], wg[:, 2 * D:], row(glu_b[l]), conv_c_w[l],
                       row(conv_c_b[l]), row(ln_c_g[l]), row(ln_c_b[l]), w_c_out[l].astype(BF16))
        x1, h2, idx, prob, rank, counts = _out_router(
            ma, mb, mc, x, g1, w_o[l].astype(BF16), row(norm2_g[l]), sh2, sc2, rw_pad[l], rb_pad[l])
        idx = idx.reshape(N, LANES)[:, :TOP_K]
        rank = rank.reshape(N, LANES)[:, :TOP_K]
        dest, tok_of_slot, blk_expert = _routing_tables(counts[0, :N_EXPERTS], idx, rank, N)
        ybuf = _experts(h2.reshape(N, D), tok_of_slot, blk_expert,
                        exp_w1, exp_b1, exp_w2, exp_b2, layer=l)
        x = _combine(ybuf, dest, x1.reshape(N, D), g2, prob.reshape(N, LANES), row(final_g),
                     S, final_norm=(l == L - 1)).reshape(B, S, D)
    return x
```

```python
import functools

import jax
import jax.numpy as jnp
from jax import lax
from jax.experimental import pallas as pl
from jax.experimental.pallas import tpu as pltpu
from jax.experimental.pallas import tpu_sc as plsc

F32 = jnp.float32
BF16 = jnp.bfloat16

EPS = 1e-6
SHORT_CONV_W = 3
RET_HEADS = 8
RET_HEAD_DIM = 128
RET_CHUNK = 128
ROPE_BASE = 10000.0
CONF_KERNEL = 31
N_EXPERTS = 32
TOP_K = 4
SWIGLU_LIMIT = 7.0
SWIGLU_ALPHA = 1.702

LANES = 128
SUBLANES = 8
VMEM_LIMIT = 56 * 1024 * 1024

ROW_TILE = 512
EXPERT_BLOCK = 256
SC_GATHER_WINDOW = 128
MOE_SPLIT = 4
SC_SUBCORES = 32
SC_GATHER_ROWS = SC_GATHER_WINDOW * SC_SUBCORES
COMBINE_TILE = 256
CONV_ROWS = 32
CONF_HALO = 32


def _params(n_grid):
    return pltpu.CompilerParams(
        dimension_semantics=("arbitrary",) * n_grid, vmem_limit_bytes=VMEM_LIMIT)


def _const_spec(shape):
    nd = len(shape)
    return pl.BlockSpec(shape, lambda *_: (0,) * nd, pipeline_mode=pl.Buffered(1))


def _sigmoid(x):
    return 1.0 / (1.0 + jnp.exp(-x))


def _silu(x):
    return x * _sigmoid(x)


def _norm_mod(x, g, sh, sc):
    y = x * lax.rsqrt(jnp.mean(x * x, axis=-1, keepdims=True) + EPS)
    return (y * g) * (1.0 + sc) + sh


def _bdot(a, b):
    return jnp.dot(a.astype(BF16), b.astype(BF16), preferred_element_type=F32)


def _mod_kernel(c_ref, w_ref, b_ref, o_ref):
    cond = _silu(c_ref[...])
    o_ref[0] = jnp.dot(cond, w_ref[0], preferred_element_type=F32,
                       precision=lax.Precision.HIGHEST) + b_ref[0]


def _modulation(c, mod_w, mod_b):
    L, D, M = mod_w.shape
    B = c.shape[0]
    tn = 1024
    return pl.pallas_call(
        _mod_kernel,
        grid=(L, M // tn),
        in_specs=[pl.BlockSpec((B, D), lambda l, j: (0, 0)),
                  pl.BlockSpec((1, D, tn), lambda l, j: (l, 0, j)),
                  pl.BlockSpec((1, 1, tn), lambda l, j: (l, 0, j))],
        out_specs=pl.BlockSpec((1, B, tn), lambda l, j: (l, 0, j)),
        out_shape=jax.ShapeDtypeStruct((L, B, M), F32),
        compiler_params=_params(2),
        name="modulation",
    )(c, mod_w, mod_b.reshape(L, 1, M))


def _rope_kernel(pos_ref, invf_ref, cos_ref, sin_ref):
    ang = pos_ref[0].astype(F32) * invf_ref[...]
    lane = lax.broadcasted_iota(jnp.int32, ang.shape, 1)
    s = jnp.sin(ang)
    cos_ref[0] = jnp.cos(ang)
    sin_ref[0] = jnp.where(lane < RET_HEAD_DIM // 2, -s, s)


def _rope_tables(positions):
    B, S = positions.shape
    inv_freq = 1.0 / (ROPE_BASE ** (jnp.arange(0, RET_HEAD_DIM, 2, dtype=F32) / RET_HEAD_DIM))
    invf = jnp.concatenate([inv_freq, inv_freq]).reshape(1, RET_HEAD_DIM)
    ts = min(S, 1024)
    spec = pl.BlockSpec((1, ts, RET_HEAD_DIM), lambda b, j: (b, j, 0))
    return pl.pallas_call(
        _rope_kernel,
        grid=(B, S // ts),
        in_specs=[pl.BlockSpec((1, ts, 1), lambda b, j: (b, j, 0)),
                  pl.BlockSpec((1, RET_HEAD_DIM), lambda b, j: (0, 0))],
        out_specs=[spec, spec],
        out_shape=[jax.ShapeDtypeStruct((B, S, RET_HEAD_DIM), F32)] * 2,
        compiler_params=_params(2),
        name="rope_tables",
    )(positions.reshape(B, S, 1), invf)


def _branch_a_kernel(x_ref, g_ref, sh_ref, sc_ref, w_ref, wg_ref, cw_ref, wo_ref, o_ref, buf_ref):
    tm = x_ref.shape[1]
    da = wo_ref.shape[0]

    @pl.when(pl.program_id(1) == 0)
    def _():
        buf_ref[0:SUBLANES, :] = jnp.zeros((SUBLANES, da), F32)

    h = _norm_mod(x_ref[0], g_ref[...], sh_ref[0], sc_ref[0]).astype(BF16)
    p = jnp.dot(h, w_ref[...], preferred_element_type=F32)
    b_g, c_g, xa = p[:, :da], p[:, da:2 * da], p[:, 2 * da:]
    u = c_g * xa
    buf_ref[SUBLANES:SUBLANES + tm, :] = u
    conv = cw_ref[2:3, :] * u
    for j in range(SHORT_CONV_W - 1):
        off = SUBLANES - (SHORT_CONV_W - 1) + j
        conv = conv + cw_ref[j:j + 1, :] * buf_ref[off:off + tm, :]
    buf_ref[0:SUBLANES, :] = buf_ref[tm:tm + SUBLANES, :]
    y_a = _bdot(b_g * conv, wo_ref[...])
    gate = _sigmoid(jnp.dot(h, wg_ref[...], preferred_element_type=F32))
    o_ref[0] = (gate * y_a).astype(BF16)


def _branch_a(x, g, sh, sc, w, wg, cw, wo):
    B, S, D = x.shape
    da = wo.shape[0]
    tm = min(ROW_TILE, S)
    vec = pl.BlockSpec((1, 1, D), lambda b, j: (b, 0, 0))
    return pl.pallas_call(
        _branch_a_kernel,
        grid=(B, S // tm),
        in_specs=[pl.BlockSpec((1, tm, D), lambda b, j: (b, j, 0)),
                  _const_spec((1, D)), vec, vec,
                  _const_spec(w.shape), _const_spec(wg.shape), _const_spec(cw.shape),
                  _const_spec(wo.shape)],
        out_specs=pl.BlockSpec((1, tm, D), lambda b, j: (b, j, 0)),
        out_shape=jax.ShapeDtypeStruct((B, S, D), BF16),
        scratch_shapes=[pltpu.VMEM((tm + SUBLANES, da), F32)],
        compiler_params=_params(2),
        name="branch_a",
    )(x, g, sh, sc, w, wg, cw, wo)


def _branch_c_kernel(x_ref, g_ref, sh_ref, sc_ref, w_ref, wg_ref, glub_ref, cw_ref, cb_ref,
                     lng_ref, lnb_ref, wo_ref, o_ref, buf_ref, conv_ref):
    tm = x_ref.shape[1]
    dc = wo_ref.shape[0]

    @pl.when(pl.program_id(1) == 0)
    def _():
        buf_ref[0, 0:CONF_HALO, :] = jnp.zeros((CONF_HALO, dc), F32)

    h = _norm_mod(x_ref[0], g_ref[...], sh_ref[0], sc_ref[0]).astype(BF16)
    p = jnp.dot(h, w_ref[...], preferred_element_type=F32) + glub_ref[...]
    buf_ref[0, CONF_HALO:CONF_HALO + tm, :] = p[:, :dc] * _sigmoid(p[:, dc:])
    for b in range(1, SUBLANES):
        buf_ref[b, SUBLANES:, :] = buf_ref[0, SUBLANES - b:tm + CONF_HALO - b, :]

    base = CONF_HALO - (CONF_KERNEL - 1)

    def strip(s, carry):
        r0 = pl.multiple_of(s * CONV_ROWS, CONV_ROWS)
        acc = jnp.zeros((CONV_ROWS, dc), F32)
        for j in range(CONF_KERNEL):
            a = -(-(base + j) // SUBLANES)
            b = a * SUBLANES - (base + j)
            acc = acc + cw_ref[j:j + 1, :] * buf_ref[b, pl.ds(r0 + a * SUBLANES, CONV_ROWS), :]
        conv_ref[pl.ds(r0, CONV_ROWS), :] = acc
        return carry

    lax.fori_loop(0, tm // CONV_ROWS, strip, 0)
    buf_ref[0, 0:CONF_HALO, :] = buf_ref[0, tm:tm + CONF_HALO, :]

    u = conv_ref[...] + cb_ref[...]
    mu = jnp.mean(u, axis=-1, keepdims=True)
    d = u - mu
    var = jnp.mean(d * d, axis=-1, keepdims=True)
    y = d * lax.rsqrt(var + EPS) * lng_ref[...] + lnb_ref[...]
    y_c = _bdot(_silu(y), wo_ref[...])
    gate = _sigmoid(jnp.dot(h, wg_ref[...], preferred_element_type=F32))
    o_ref[0] = (gate * y_c).astype(BF16)


def _branch_c(x, g, sh, sc, w, wg, glub, cw, cb, lng, lnb, wo):
    B, S, D = x.shape
    dc = wo.shape[0]
    tm = min(ROW_TILE, S)
    vec = pl.BlockSpec((1, 1, D), lambda b, j: (b, 0, 0))
    return pl.pallas_call(
        _branch_c_kernel,
        grid=(B, S // tm),
        in_specs=[pl.BlockSpec((1, tm, D), lambda b, j: (b, j, 0)),
                  _const_spec((1, D)), vec, vec,
                  _const_spec(w.shape), _const_spec(wg.shape), _const_spec(glub.shape),
                  _const_spec(cw.shape), _const_spec(cb.shape), _const_spec(lng.shape),
                  _const_spec(lnb.shape), _const_spec(wo.shape)],
        out_specs=pl.BlockSpec((1, tm, D), lambda b, j: (b, j, 0)),
        out_shape=jax.ShapeDtypeStruct((B, S, D), BF16),
        scratch_shapes=[pltpu.VMEM((SUBLANES, tm + CONF_HALO, dc), F32), pltpu.VMEM((tm, dc), F32)],
        compiler_params=_params(2),
        name="branch_c",
    )(x, g, sh, sc, w, wg, glub, cw, cb, lng, lnb, wo)


def _branch_b_kernel(x_ref, g_ref, sh_ref, sc_ref, cos_ref, sin_ref, w_ref, wg_ref,
                     dmask_ref, xi_ref, zeta_ref, gch_ref, rg_ref, wo_ref, o_ref,
                     q_ref, k_ref, v_ref, r_ref, state_ref):
    tm = x_ref.shape[1]
    dr = wo_ref.shape[0]
    hd = RET_HEAD_DIM

    @pl.when(pl.program_id(1) == 0)
    def _():
        state_ref[...] = jnp.zeros(state_ref.shape, F32)

    h = _norm_mod(x_ref[0], g_ref[...], sh_ref[0], sc_ref[0]).astype(BF16)
    cos = cos_ref[0]
    sin = sin_ref[0]
    scale = hd ** -0.5

    def rope(t):
        return t * cos + pltpu.roll(t, hd // 2, axis=1) * sin

    q = jnp.dot(h, w_ref[:, 0:dr], preferred_element_type=F32) * scale
    k = jnp.dot(h, w_ref[:, dr:2 * dr], preferred_element_type=F32)
    for hh in range(RET_HEADS):
        sl = slice(hh * hd, (hh + 1) * hd)
        q_ref[:, sl] = rope(q[:, sl]).astype(BF16)
        k_ref[:, sl] = rope(k[:, sl])
    v_ref[...] = jnp.dot(h, w_ref[:, 2 * dr:3 * dr], preferred_element_type=F32).astype(BF16)

    def chunk(ci, carry):
        r0 = pl.multiple_of(ci * RET_CHUNK, RET_CHUNK)
        rows = pl.ds(r0, RET_CHUNK)
        for hh in range(RET_HEADS):
            sl = slice(hh * hd, (hh + 1) * hd)
            qh = q_ref[rows, sl]
            kh = k_ref[rows, sl]
            vh = v_ref[rows, sl]
            st = state_ref[hh]
            scores = lax.dot_general(qh, kh.astype(BF16), (((1,), (1,)), ((), ())),
                                     preferred_element_type=F32) * dmask_ref[hh]
            inner = _bdot(scores, vh)
            cross = _bdot(qh, st) * xi_ref[hh]
            kz = (kh * zeta_ref[hh]).astype(BF16)
            state_ref[hh] = st * gch_ref[hh] + lax.dot_general(
                kz, vh, (((0,), (0,)), ((), ())), preferred_element_type=F32)
            o = inner + cross
            mu = jnp.mean(o, axis=-1, keepdims=True)
            d = o - mu
            var = jnp.mean(d * d, axis=-1, keepdims=True)
            r_ref[rows, sl] = d * lax.rsqrt(var + EPS)
        return carry

    lax.fori_loop(0, tm // RET_CHUNK, chunk, 0)

    og = jnp.dot(h, w_ref[:, 3 * dr:4 * dr], preferred_element_type=F32)
    y_b = _bdot(_silu(og) * (r_ref[...] * rg_ref[...]), wo_ref[...])
    gate = _sigmoid(jnp.dot(h, wg_ref[...], preferred_element_type=F32))
    o_ref[0] = (gate * y_b).astype(BF16)


def _retention_tables():
    H, C = RET_HEADS, RET_CHUNK
    gamma = 1.0 - jnp.power(2.0, -5.0 - jnp.arange(H, dtype=F32))
    lg = jnp.log(gamma)
    idx = jnp.arange(C, dtype=F32)
    rel = idx[:, None] - idx[None, :]
    dmask = jnp.where(rel[None] >= 0, jnp.exp(jnp.maximum(rel, 0.0)[None] * lg[:, None, None]), 0.0)
    xi = jnp.exp((idx + 1.0)[None] * lg[:, None])
    zeta = jnp.exp((C - 1.0 - idx)[None] * lg[:, None])
    g_chunk = jnp.exp(C * lg)
    bc = lambda t: jnp.broadcast_to(t[:, :, None], (H, C, RET_HEAD_DIM)).astype(F32)
    g_rows = jnp.broadcast_to(g_chunk[:, None, None], (H, 1, RET_HEAD_DIM)).astype(F32)
    return dmask.astype(F32), bc(xi), bc(zeta), g_rows


def _branch_b(x, g, sh, sc, cosf, sinf, w, wg, tables, rg, wo):
    B, S, D = x.shape
    dr = wo.shape[0]
    tm = min(ROW_TILE, S)
    dmask, xi, zeta, g_chunk = tables
    vec = pl.BlockSpec((1, 1, D), lambda b, j: (b, 0, 0))
    rope_spec = pl.BlockSpec((1, tm, RET_HEAD_DIM), lambda b, j: (b, j, 0))
    return pl.pallas_call(
        _branch_b_kernel,
        grid=(B, S // tm),
        in_specs=[pl.BlockSpec((1, tm, D), lambda b, j: (b, j, 0)),
                  _const_spec((1, D)), vec, vec, rope_spec, rope_spec,
                  _const_spec(w.shape), _const_spec(wg.shape), _const_spec(dmask.shape),
                  _const_spec(xi.shape), _const_spec(zeta.shape), _const_spec(g_chunk.shape),
                  _const_spec(rg.shape), _const_spec(wo.shape)],
        out_specs=pl.BlockSpec((1, tm, D), lambda b, j: (b, j, 0)),
        out_shape=jax.ShapeDtypeStruct((B, S, D), BF16),
        scratch_shapes=[pltpu.VMEM((tm, dr), BF16), pltpu.VMEM((tm, dr), F32),
                        pltpu.VMEM((tm, dr), BF16), pltpu.VMEM((tm, dr), F32),
                        pltpu.VMEM((RET_HEADS, RET_HEAD_DIM, RET_HEAD_DIM), F32)],
        compiler_params=_params(2),
        name="branch_b",
    )(x, g, sh, sc, cosf, sinf, w, wg, dmask, xi, zeta, g_chunk, rg, wo)


def _out_router_kernel(ma_ref, mb_ref, mc_ref, x_ref, g1_ref, wo_ref, n2_ref, sh_ref, sc_ref,
                       rw_ref, rb_ref, x1_ref, idx_ref, prob_ref, rank_ref, cnt_ref, *rest):
    h2_refs, carry_ref = rest[:MOE_SPLIT], rest[MOE_SPLIT]
    tm = x_ref.shape[1]
    first = jnp.logical_and(pl.program_id(0) == 0, pl.program_id(1) == 0)

    @pl.when(first)
    def _():
        carry_ref[...] = jnp.zeros(carry_ref.shape, F32)

    m = ma_ref[0].astype(F32) + mb_ref[0].astype(F32) + mc_ref[0].astype(F32)
    x1 = x_ref[0] + g1_ref[0] * _bdot(m, wo_ref[...])
    x1_ref[0] = x1
    h2 = _norm_mod(x1, n2_ref[...], sh_ref[0], sc_ref[0])
    dq = h2.shape[1] // MOE_SPLIT
    for q, h2_ref in enumerate(h2_refs):
        h2_ref[...] = h2[:, q * dq:(q + 1) * dq]

    logits = jnp.dot(h2, rw_ref[...], preferred_element_type=F32,
                     precision=lax.Precision.HIGHEST) + rb_ref[...]
    lane = lax.broadcasted_iota(jnp.int32, logits.shape, 1)
    lane_f = lane.astype(F32)
    neg = jnp.float32(-jnp.inf)
    work = jnp.where(lane < N_EXPERTS, logits, neg)
    idx_out = jnp.zeros(logits.shape, jnp.int32)
    val_out = jnp.full(logits.shape, neg, F32)
    onehot_all = jnp.zeros(logits.shape, F32)
    sels = []
    for kk in range(TOP_K):
        mx = jnp.max(work, axis=-1, keepdims=True)
        first_idx = jnp.min(jnp.where(work == mx, lane_f, float(LANES)), axis=-1, keepdims=True)
        sel = lane_f == first_idx
        sels.append(sel)
        idx_out = jnp.where(lane == kk, first_idx.astype(jnp.int32), idx_out)
        val_out = jnp.where(lane == kk, mx, val_out)
        onehot_all = onehot_all + sel.astype(F32)
        work = jnp.where(sel, neg, work)

    ex = jnp.exp(val_out - val_out[:, 0:1])
    prob_ref[0] = ex / jnp.sum(ex, axis=-1, keepdims=True)
    idx_ref[0] = idx_out

    row = lax.broadcasted_iota(jnp.int32, (tm, tm), 0)
    col = lax.broadcasted_iota(jnp.int32, (tm, tm), 1)
    lower = (col < row).astype(BF16)
    before = jnp.dot(lower, onehot_all.astype(BF16), preferred_element_type=F32) + carry_ref[...]
    rank_out = jnp.zeros(logits.shape, jnp.int32)
    for kk in range(TOP_K):
        rk = jnp.sum(jnp.where(sels[kk], before, 0.0), axis=-1, keepdims=True)
        rank_out = jnp.where(lane == kk, rk.astype(jnp.int32), rank_out)
    rank_ref[0] = rank_out
    total = carry_ref[...] + jnp.sum(onehot_all, axis=0, keepdims=True)
    carry_ref[...] = total
    cnt_ref[...] = total


def _out_router(ma, mb, mc, x, g1, wo, n2, sh2, sc2, rw, rb):
    B, S, D = x.shape
    tm = min(ROW_TILE, S)
    tile = pl.BlockSpec((1, tm, D), lambda b, j: (b, j, 0))
    vec = pl.BlockSpec((1, 1, D), lambda b, j: (b, 0, 0))
    lanes = pl.BlockSpec((1, tm, LANES), lambda b, j: (b, j, 0))
    dq = D // MOE_SPLIT
    per_seq = S // tm
    part = pl.BlockSpec((tm, dq), lambda b, j: (b * per_seq + j, 0))
    return pl.pallas_call(
        _out_router_kernel,
        grid=(B, S // tm),
        in_specs=[tile, tile, tile, tile, vec, _const_spec(wo.shape), _const_spec((1, D)),
                  vec, vec, _const_spec(rw.shape), _const_spec(rb.shape)],
        out_specs=[tile, lanes, lanes, lanes, pl.BlockSpec((1, LANES), lambda b, j: (0, 0))]
        + [part] * MOE_SPLIT,
        out_shape=[jax.ShapeDtypeStruct((B, S, D), F32),
                   jax.ShapeDtypeStruct((B, S, LANES), jnp.int32),
                   jax.ShapeDtypeStruct((B, S, LANES), F32),
                   jax.ShapeDtypeStruct((B, S, LANES), jnp.int32),
                   jax.ShapeDtypeStruct((1, LANES), F32)]
        + [jax.ShapeDtypeStruct((B * S, dq), F32)] * MOE_SPLIT,
        scratch_shapes=[pltpu.VMEM((1, LANES), F32)],
        compiler_params=_params(2),
        name="out_router",
    )(ma, mb, mc, x, g1, wo, n2, sh2, sc2, rw, rb)


def _sc_row_gather(src, indices):
    n = indices.shape[0]
    d = src.shape[1]
    w = SC_GATHER_WINDOW
    mesh = plsc.VectorSubcoreMesh(core_axis_name="core", subcore_axis_name="subcore")

    @pl.kernel(out_type=jax.ShapeDtypeStruct((n, d), src.dtype), mesh=mesh, scratch_types=[])
    def gather_kernel(src_hbm, idx_hbm, out_hbm):
        def body(idx_vmem, out_vmem):
            pltpu.sync_copy(src_hbm.at[idx_vmem.at[0]], out_vmem)

        pltpu.emit_pipeline(
            body,
            grid=(n // w,),
            in_specs=[pl.BlockSpec((1, w), index_map=lambda i: (0, i))],
            out_specs=[pl.BlockSpec((w, d), index_map=lambda i: (i, 0))],
            core_axis_name=("core", "subcore"),
            dimension_semantics=(pltpu.PARALLEL,),
        )(idx_hbm, out_hbm)

    return gather_kernel(src, indices.reshape(1, n))


def _expert_kernel(be_ref, *refs):
    x_refs, refs = refs[:MOE_SPLIT], refs[MOE_SPLIT:]
    w1_ref, b1_ref, w2_ref, b2_ref = refs[:4]
    o_refs, (w1b_ref, w2b_ref) = refs[4:4 + MOE_SPLIT], refs[4 + MOE_SPLIT:]
    i = pl.program_id(0)
    f = w2_ref.shape[2]
    dq = x_refs[0].shape[1]

    @pl.when(jnp.logical_or(i == 0, be_ref[i] != be_ref[jnp.maximum(i - 1, 0)]))
    def _():
        w1b_ref[...] = w1_ref[0, 0].astype(BF16)
        w2b_ref[...] = w2_ref[0, 0].astype(BF16)

    gu = b1_ref[0, 0]
    for q, x_ref in enumerate(x_refs):
        gu = gu + jnp.dot(x_ref[...].astype(BF16), w1b_ref[q * dq:(q + 1) * dq, :],
                          preferred_element_type=F32)
    glu = jnp.minimum(gu[:, :f], SWIGLU_LIMIT)
    lin = jnp.clip(gu[:, f:], -SWIGLU_LIMIT, SWIGLU_LIMIT)
    act = glu * _sigmoid(SWIGLU_ALPHA * glu) * (lin + 1.0)
    y = _bdot(act, w2b_ref[...]) + b2_ref[0, 0]
    for q, o_ref in enumerate(o_refs):
        o_ref[...] = y[:, q * dq:(q + 1) * dq]


def _experts(xparts, blk_expert, w1, b1, w2, b2, layer):
    rows, dq = xparts[0].shape
    L, E, D, F2 = w1.shape
    nb = blk_expert.shape[0]
    blk = EXPERT_BLOCK
    per_expert = lambda shape: pl.BlockSpec((1, 1) + shape, lambda i, be: (layer, be[i], 0, 0))
    grid_spec = pltpu.PrefetchScalarGridSpec(
        num_scalar_prefetch=1,
        grid=(nb,),
        in_specs=[pl.BlockSpec((blk, dq), lambda i, be: (i, 0))] * MOE_SPLIT
        + [per_expert((D, F2)), per_expert((1, F2)), per_expert((F2 // 2, D)), per_expert((1, D))],
        out_specs=[pl.BlockSpec((blk, dq), lambda i, be: (i, 0))] * MOE_SPLIT,
        scratch_shapes=[pltpu.VMEM((D, F2), BF16), pltpu.VMEM((F2 // 2, D), BF16)])
    return pl.pallas_call(
        _expert_kernel,
        grid_spec=grid_spec,
        out_shape=[jax.ShapeDtypeStruct((rows, dq), F32)] * MOE_SPLIT,
        compiler_params=_params(1),
        name="experts",
    )(blk_expert, *xparts, w1, b1.reshape(L, E, 1, F2), w2, b2.reshape(L, E, 1, D))


def _combine_kernel(*refs, final_norm):
    y_refs, (x1_ref, g2_ref, prob_ref, fg_ref, o_ref) = refs[:MOE_SPLIT], refs[MOE_SPLIT:]
    tm = x1_ref.shape[0]
    prob = prob_ref[...]
    parts = []
    for y_ref in y_refs:
        acc = prob[:, 0:1] * y_ref[0:tm, :]
        for kk in range(1, TOP_K):
            acc = acc + prob[:, kk:kk + 1] * y_ref[kk * tm:(kk + 1) * tm, :]
        parts.append(acc)
    x2 = x1_ref[...] + g2_ref[0] * jnp.concatenate(parts, axis=1)
    if final_norm:
        x2 = x2 * lax.rsqrt(jnp.mean(x2 * x2, axis=-1, keepdims=True) + EPS) * fg_ref[...]
    o_ref[...] = x2


def _combine(yparts, x1, g2, prob, final_g, seq_len, final_norm):
    N, D = x1.shape
    tm = min(COMBINE_TILE, seq_len)
    per_seq = seq_len // tm
    return pl.pallas_call(
        functools.partial(_combine_kernel, final_norm=final_norm),
        grid=(N // tm,),
        in_specs=[pl.BlockSpec((TOP_K * tm, D // MOE_SPLIT), lambda i: (i, 0))] * MOE_SPLIT
        + [pl.BlockSpec((tm, D), lambda i: (i, 0)),
                  pl.BlockSpec((1, 1, D), lambda i: (i // per_seq, 0, 0)),
                  pl.BlockSpec((tm, LANES), lambda i: (i, 0)),
                  pl.BlockSpec((1, D), lambda i: (0, 0))],
        out_specs=pl.BlockSpec((tm, D), lambda i: (i, 0)),
        out_shape=jax.ShapeDtypeStruct((N, D), F32),
        compiler_params=_params(1),
        name="combine",
    )(*yparts, x1, g2, prob, final_g)


def _routing_tables(counts, idx, rank, n_tokens):
    blk = EXPERT_BLOCK
    A = n_tokens * TOP_K
    nb = (A + N_EXPERTS * (blk - 1)) // blk
    group = max(1, SC_GATHER_ROWS // blk)
    nb = -(-nb // group) * group
    counts = counts.astype(jnp.int32)
    padded = ((counts + blk - 1) // blk) * blk
    pends = jnp.cumsum(padded)
    pstarts = pends - padded
    block_start = jnp.arange(nb, dtype=jnp.int32) * blk
    blk_expert = jnp.minimum(jnp.sum(block_start[:, None] >= pends[None, :], axis=1),
                             N_EXPERTS - 1).astype(jnp.int32)
    dest = pstarts[idx] + rank
    tok = jnp.repeat(jnp.arange(n_tokens, dtype=jnp.int32), TOP_K)
    n_pad = nb * blk - A
    pad_per = padded - counts
    padcum = jnp.cumsum(pad_per)
    q = jnp.arange(n_pad, dtype=jnp.int32)
    e_q = jnp.sum(q[:, None] >= padcum[None, :], axis=1).astype(jnp.int32)
    e_c = jnp.minimum(e_q, N_EXPERTS - 1)
    key_in = (pstarts + counts - (padcum - pad_per))[e_c] + q
    key_tail = pends[-1] - padcum[-1] + q
    pad_keys = jnp.where(e_q < N_EXPERTS, key_in, key_tail)
    keys = jnp.concatenate([dest.reshape(A), pad_keys])
    vals = jnp.concatenate([tok, jnp.zeros((n_pad,), jnp.int32)])
    _, tok_of_slot = lax.sort((keys, vals), num_keys=1)
    return dest, tok_of_slot, blk_expert


def kernel(x, c, positions, mod_w, mod_b, norm1_g, w_in, conv_a_w, w_a_out, ret_norm_g, w_b_out, glu_b, conv_c_w, conv_c_b, ln_c_g, ln_c_b, w_c_out, w_o, norm2_g, router_w, router_b, exp_w1, exp_b1, exp_w2, exp_b2, final_g):
    B, S, D = x.shape
    L = mod_w.shape[0]
    N = B * S
    da = w_a_out.shape[1]
    dr = w_b_out.shape[1]
    dc = w_c_out.shape[1]
    s1 = 3 * da
    s2 = s1 + 4 * dr
    s3 = s2 + 2 * dc

    mod = _modulation(c, mod_w, mod_b)
    cosf, sinf = _rope_tables(positions)
    tables = _retention_tables()
    row = lambda v: v.reshape(1, -1)
    rw_pad = jnp.zeros((L, D, LANES), F32).at[:, :, :N_EXPERTS].set(router_w)
    rb_pad = jnp.zeros((L, 1, LANES), F32).at[:, 0, :N_EXPERTS].set(router_b)

    for l in range(L):
        sh1, sc1, g1, sh2, sc2, g2 = [mod[l, :, i * D:(i + 1) * D].reshape(B, 1, D) for i in range(6)]
        wl = w_in[l].astype(BF16)
        wg = wl[:, s3:]
        n1 = row(norm1_g[l])
        ma = _branch_a(x, n1, sh1, sc1, wl[:, :s1], wg[:, :D], conv_a_w[l],
                       w_a_out[l].astype(BF16))
        mb = _branch_b(x, n1, sh1, sc1, cosf, sinf, wl[:, s1:s2], wg[:, D:2 * D], tables,
                       row(ret_norm_g[l]), w_b_out[l].astype(BF16))
        mc = _branch_c(x, n1, sh1, sc1, wl[:, s2:s3], wg[:, 2 * D:], row(glu_b[l]), conv_c_w[l],
                       row(conv_c_b[l]), row(ln_c_g[l]), row(ln_c_b[l]), w_c_out[l].astype(BF16))
        x1, idx, prob, rank, counts, *h2_parts = _out_router(
            ma, mb, mc, x, g1, w_o[l].astype(BF16), row(norm2_g[l]), sh2, sc2, rw_pad[l], rb_pad[l])
        idx = idx.reshape(N, LANES)[:, :TOP_K]
        rank = rank.reshape(N, LANES)[:, :TOP_K]
        dest, tok_of_slot, blk_expert = _routing_tables(counts[0, :N_EXPERTS], idx, rank, N)
        xparts = [_sc_row_gather(p, tok_of_slot) for p in h2_parts]
        yparts = _experts(xparts, blk_expert, exp_w1, exp_b1, exp_w2, exp_b2, layer=l)
        tmc = min(COMBINE_TILE, S)
        dest_kmajor = dest.reshape(N // tmc, tmc, TOP_K).transpose(0, 2, 1).reshape(N * TOP_K)
        ycat = [_sc_row_gather(p, dest_kmajor) for p in yparts]
        x = _combine(ycat, x1.reshape(N, D), g2, prob.reshape(N, LANES), row(final_g),
                     S, final_norm=(l == L - 1)).reshape(B, S, D)
    return x
```

```python
import functools

import jax
import jax.numpy as jnp
from jax import lax
from jax.experimental import pallas as pl
from jax.experimental.pallas import tpu as pltpu
from jax.experimental.pallas import tpu_sc as plsc

F32 = jnp.float32
BF16 = jnp.bfloat16

EPS = 1e-6
SHORT_CONV_W = 3
RET_HEADS = 8
RET_HEAD_DIM = 128
RET_CHUNK = 128
ROPE_BASE = 10000.0
CONF_KERNEL = 31
N_EXPERTS = 32
TOP_K = 4
SWIGLU_LIMIT = 7.0
SWIGLU_ALPHA = 1.702

LANES = 128
SUBLANES = 8
VMEM_LIMIT = 56 * 1024 * 1024

ROW_TILE = 512
EXPERT_BLOCK = 256
SC_GATHER_WINDOW = 128
MOE_SPLIT = 2
SC_SUBCORES = 32
SC_GATHER_ROWS = SC_GATHER_WINDOW * SC_SUBCORES
COMBINE_TILE = 256
CONV_ROWS = 32
CONF_HALO = 32


def _params(n_grid):
    return pltpu.CompilerParams(
        dimension_semantics=("arbitrary",) * n_grid, vmem_limit_bytes=VMEM_LIMIT)


def _const_spec(shape):
    nd = len(shape)
    return pl.BlockSpec(shape, lambda *_: (0,) * nd, pipeline_mode=pl.Buffered(1))


def _sigmoid(x):
    return 1.0 / (1.0 + jnp.exp(-x))


def _silu(x):
    return x * _sigmoid(x)


def _norm_mod(x, g, sh, sc):
    y = x * lax.rsqrt(jnp.mean(x * x, axis=-1, keepdims=True) + EPS)
    return (y * g) * (1.0 + sc) + sh


def _bdot(a, b):
    return jnp.dot(a.astype(BF16), b.astype(BF16), preferred_element_type=F32)


def _pack_rows(v):
    c = v.shape[1] // MOE_SPLIT
    parts = []
    for p in range(MOE_SPLIT):
        lo = v[:, p * c:p * c + c // 2].astype(BF16).astype(F32)
        hi = v[:, p * c + c // 2:(p + 1) * c].astype(BF16).astype(F32)
        lo_bits = lax.bitcast_convert_type(lo, jnp.uint32) >> 16
        hi_bits = lax.bitcast_convert_type(hi, jnp.uint32) & jnp.uint32(0xFFFF0000)
        parts.append(hi_bits | lo_bits)
    return parts


def _unpack_rows(parts):
    cols = []
    for w in parts:
        cols.append(lax.bitcast_convert_type(w << 16, F32))
        cols.append(lax.bitcast_convert_type(w & jnp.uint32(0xFFFF0000), F32))
    return jnp.concatenate(cols, axis=1)


def _mod_kernel(c_ref, w_ref, b_ref, o_ref):
    cond = _silu(c_ref[...])
    o_ref[0] = jnp.dot(cond, w_ref[0], preferred_element_type=F32,
                       precision=lax.Precision.HIGHEST) + b_ref[0]


def _modulation(c, mod_w, mod_b):
    L, D, M = mod_w.shape
    B = c.shape[0]
    tn = 1024
    return pl.pallas_call(
        _mod_kernel,
        grid=(L, M // tn),
        in_specs=[pl.BlockSpec((B, D), lambda l, j: (0, 0)),
                  pl.BlockSpec((1, D, tn), lambda l, j: (l, 0, j)),
                  pl.BlockSpec((1, 1, tn), lambda l, j: (l, 0, j))],
        out_specs=pl.BlockSpec((1, B, tn), lambda l, j: (l, 0, j)),
        out_shape=jax.ShapeDtypeStruct((L, B, M), F32),
        compiler_params=_params(2),
        name="modulation",
    )(c, mod_w, mod_b.reshape(L, 1, M))


def _rope_kernel(pos_ref, invf_ref, cos_ref, sin_ref):
    ang = pos_ref[0].astype(F32) * invf_ref[...]
    lane = lax.broadcasted_iota(jnp.int32, ang.shape, 1)
    s = jnp.sin(ang)
    cos_ref[0] = jnp.cos(ang)
    sin_ref[0] = jnp.where(lane < RET_HEAD_DIM // 2, -s, s)


def _rope_tables(positions):
    B, S = positions.shape
    inv_freq = 1.0 / (ROPE_BASE ** (jnp.arange(0, RET_HEAD_DIM, 2, dtype=F32) / RET_HEAD_DIM))
    invf = jnp.concatenate([inv_freq, inv_freq]).reshape(1, RET_HEAD_DIM)
    ts = min(S, 1024)
    spec = pl.BlockSpec((1, ts, RET_HEAD_DIM), lambda b, j: (b, j, 0))
    return pl.pallas_call(
        _rope_kernel,
        grid=(B, S // ts),
        in_specs=[pl.BlockSpec((1, ts, 1), lambda b, j: (b, j, 0)),
                  pl.BlockSpec((1, RET_HEAD_DIM), lambda b, j: (0, 0))],
        out_specs=[spec, spec],
        out_shape=[jax.ShapeDtypeStruct((B, S, RET_HEAD_DIM), F32)] * 2,
        compiler_params=_params(2),
        name="rope_tables",
    )(positions.reshape(B, S, 1), invf)


def _branch_a_kernel(x_ref, g_ref, sh_ref, sc_ref, w_ref, wg_ref, cw_ref, wo_ref, o_ref, buf_ref):
    tm = x_ref.shape[1]
    da = wo_ref.shape[0]

    @pl.when(pl.program_id(1) == 0)
    def _():
        buf_ref[0:SUBLANES, :] = jnp.zeros((SUBLANES, da), F32)

    h = _norm_mod(x_ref[0], g_ref[...], sh_ref[0], sc_ref[0]).astype(BF16)
    p = jnp.dot(h, w_ref[...], preferred_element_type=F32)
    b_g, c_g, xa = p[:, :da], p[:, da:2 * da], p[:, 2 * da:]
    u = c_g * xa
    buf_ref[SUBLANES:SUBLANES + tm, :] = u
    conv = cw_ref[2:3, :] * u
    for j in range(SHORT_CONV_W - 1):
        off = SUBLANES - (SHORT_CONV_W - 1) + j
        conv = conv + cw_ref[j:j + 1, :] * buf_ref[off:off + tm, :]
    buf_ref[0:SUBLANES, :] = buf_ref[tm:tm + SUBLANES, :]
    y_a = _bdot(b_g * conv, wo_ref[...])
    gate = _sigmoid(jnp.dot(h, wg_ref[...], preferred_element_type=F32))
    o_ref[0] = (gate * y_a).astype(BF16)


def _branch_a(x, g, sh, sc, w, wg, cw, wo):
    B, S, D = x.shape
    da = wo.shape[0]
    tm = min(ROW_TILE, S)
    vec = pl.BlockSpec((1, 1, D), lambda b, j: (b, 0, 0))
    return pl.pallas_call(
        _branch_a_kernel,
        grid=(B, S // tm),
        in_specs=[pl.BlockSpec((1, tm, D), lambda b, j: (b, j, 0)),
                  _const_spec((1, D)), vec, vec,
                  _const_spec(w.shape), _const_spec(wg.shape), _const_spec(cw.shape),
                  _const_spec(wo.shape)],
        out_specs=pl.BlockSpec((1, tm, D), lambda b, j: (b, j, 0)),
        out_shape=jax.ShapeDtypeStruct((B, S, D), BF16),
        scratch_shapes=[pltpu.VMEM((tm + SUBLANES, da), F32)],
        compiler_params=_params(2),
        name="branch_a",
    )(x, g, sh, sc, w, wg, cw, wo)


def _branch_c_kernel(x_ref, g_ref, sh_ref, sc_ref, w_ref, wg_ref, glub_ref, cw_ref, cb_ref,
                     lng_ref, lnb_ref, wo_ref, o_ref, buf_ref, conv_ref):
    tm = x_ref.shape[1]
    dc = wo_ref.shape[0]

    @pl.when(pl.program_id(1) == 0)
    def _():
        buf_ref[0, 0:CONF_HALO, :] = jnp.zeros((CONF_HALO, dc), F32)

    h = _norm_mod(x_ref[0], g_ref[...], sh_ref[0], sc_ref[0]).astype(BF16)
    p = jnp.dot(h, w_ref[...], preferred_element_type=F32) + glub_ref[...]
    buf_ref[0, CONF_HALO:CONF_HALO + tm, :] = p[:, :dc] * _sigmoid(p[:, dc:])
    for b in range(1, SUBLANES):
        buf_ref[b, SUBLANES:, :] = buf_ref[0, SUBLANES - b:tm + CONF_HALO - b, :]

    base = CONF_HALO - (CONF_KERNEL - 1)

    def strip(s, carry):
        r0 = pl.multiple_of(s * CONV_ROWS, CONV_ROWS)
        acc = jnp.zeros((CONV_ROWS, dc), F32)
        for j in range(CONF_KERNEL):
            a = -(-(base + j) // SUBLANES)
            b = a * SUBLANES - (base + j)
            acc = acc + cw_ref[j:j + 1, :] * buf_ref[b, pl.ds(r0 + a * SUBLANES, CONV_ROWS), :]
        conv_ref[pl.ds(r0, CONV_ROWS), :] = acc
        return carry

    lax.fori_loop(0, tm // CONV_ROWS, strip, 0)
    buf_ref[0, 0:CONF_HALO, :] = buf_ref[0, tm:tm + CONF_HALO, :]

    u = conv_ref[...] + cb_ref[...]
    mu = jnp.mean(u, axis=-1, keepdims=True)
    d = u - mu
    var = jnp.mean(d * d, axis=-1, keepdims=True)
    y = d * lax.rsqrt(var + EPS) * lng_ref[...] + lnb_ref[...]
    y_c = _bdot(_silu(y), wo_ref[...])
    gate = _sigmoid(jnp.dot(h, wg_ref[...], preferred_element_type=F32))
    o_ref[0] = (gate * y_c).astype(BF16)


def _branch_c(x, g, sh, sc, w, wg, glub, cw, cb, lng, lnb, wo):
    B, S, D = x.shape
    dc = wo.shape[0]
    tm = min(ROW_TILE, S)
    vec = pl.BlockSpec((1, 1, D), lambda b, j: (b, 0, 0))
    return pl.pallas_call(
        _branch_c_kernel,
        grid=(B, S // tm),
        in_specs=[pl.BlockSpec((1, tm, D), lambda b, j: (b, j, 0)),
                  _const_spec((1, D)), vec, vec,
                  _const_spec(w.shape), _const_spec(wg.shape), _const_spec(glub.shape),
                  _const_spec(cw.shape), _const_spec(cb.shape), _const_spec(lng.shape),
                  _const_spec(lnb.shape), _const_spec(wo.shape)],
        out_specs=pl.BlockSpec((1, tm, D), lambda b, j: (b, j, 0)),
        out_shape=jax.ShapeDtypeStruct((B, S, D), BF16),
        scratch_shapes=[pltpu.VMEM((SUBLANES, tm + CONF_HALO, dc), F32), pltpu.VMEM((tm, dc), F32)],
        compiler_params=_params(2),
        name="branch_c",
    )(x, g, sh, sc, w, wg, glub, cw, cb, lng, lnb, wo)


def _branch_b_kernel(x_ref, g_ref, sh_ref, sc_ref, cos_ref, sin_ref, w_ref, wg_ref,
                     dmask_ref, xi_ref, zeta_ref, gch_ref, rg_ref, wo_ref, o_ref,
                     q_ref, k_ref, v_ref, r_ref, state_ref):
    tm = x_ref.shape[1]
    dr = wo_ref.shape[0]
    hd = RET_HEAD_DIM

    @pl.when(pl.program_id(1) == 0)
    def _():
        state_ref[...] = jnp.zeros(state_ref.shape, F32)

    h = _norm_mod(x_ref[0], g_ref[...], sh_ref[0], sc_ref[0]).astype(BF16)
    cos = cos_ref[0]
    sin = sin_ref[0]
    scale = hd ** -0.5

    def rope(t):
        return t * cos + pltpu.roll(t, hd // 2, axis=1) * sin

    q = jnp.dot(h, w_ref[:, 0:dr], preferred_element_type=F32) * scale
    k = jnp.dot(h, w_ref[:, dr:2 * dr], preferred_element_type=F32)
    for hh in range(RET_HEADS):
        sl = slice(hh * hd, (hh + 1) * hd)
        q_ref[:, sl] = rope(q[:, sl]).astype(BF16)
        k_ref[:, sl] = rope(k[:, sl])
    v_ref[...] = jnp.dot(h, w_ref[:, 2 * dr:3 * dr], preferred_element_type=F32).astype(BF16)

    def chunk(ci, carry):
        r0 = pl.multiple_of(ci * RET_CHUNK, RET_CHUNK)
        rows = pl.ds(r0, RET_CHUNK)
        for hh in range(RET_HEADS):
            sl = slice(hh * hd, (hh + 1) * hd)
            qh = q_ref[rows, sl]
            kh = k_ref[rows, sl]
            vh = v_ref[rows, sl]
            st = state_ref[hh]
            scores = lax.dot_general(qh, kh.astype(BF16), (((1,), (1,)), ((), ())),
                                     preferred_element_type=F32) * dmask_ref[hh]
            inner = _bdot(scores, vh)
            cross = _bdot(qh, st) * xi_ref[hh]
            kz = (kh * zeta_ref[hh]).astype(BF16)
            state_ref[hh] = st * gch_ref[hh] + lax.dot_general(
                kz, vh, (((0,), (0,)), ((), ())), preferred_element_type=F32)
            o = inner + cross
            mu = jnp.mean(o, axis=-1, keepdims=True)
            d = o - mu
            var = jnp.mean(d * d, axis=-1, keepdims=True)
            r_ref[rows, sl] = d * lax.rsqrt(var + EPS)
        return carry

    lax.fori_loop(0, tm // RET_CHUNK, chunk, 0)

    og = jnp.dot(h, w_ref[:, 3 * dr:4 * dr], preferred_element_type=F32)
    y_b = _bdot(_silu(og) * (r_ref[...] * rg_ref[...]), wo_ref[...])
    gate = _sigmoid(jnp.dot(h, wg_ref[...], preferred_element_type=F32))
    o_ref[0] = (gate * y_b).astype(BF16)


def _retention_tables():
    H, C = RET_HEADS, RET_CHUNK
    gamma = 1.0 - jnp.power(2.0, -5.0 - jnp.arange(H, dtype=F32))
    lg = jnp.log(gamma)
    idx = jnp.arange(C, dtype=F32)
    rel = idx[:, None] - idx[None, :]
    dmask = jnp.where(rel[None] >= 0, jnp.exp(jnp.maximum(rel, 0.0)[None] * lg[:, None, None]), 0.0)
    xi = jnp.exp((idx + 1.0)[None] * lg[:, None])
    zeta = jnp.exp((C - 1.0 - idx)[None] * lg[:, None])
    g_chunk = jnp.exp(C * lg)
    bc = lambda t: jnp.broadcast_to(t[:, :, None], (H, C, RET_HEAD_DIM)).astype(F32)
    g_rows = jnp.broadcast_to(g_chunk[:, None, None], (H, 1, RET_HEAD_DIM)).astype(F32)
    return dmask.astype(F32), bc(xi), bc(zeta), g_rows


def _branch_b(x, g, sh, sc, cosf, sinf, w, wg, tables, rg, wo):
    B, S, D = x.shape
    dr = wo.shape[0]
    tm = min(ROW_TILE, S)
    dmask, xi, zeta, g_chunk = tables
    vec = pl.BlockSpec((1, 1, D), lambda b, j: (b, 0, 0))
    rope_spec = pl.BlockSpec((1, tm, RET_HEAD_DIM), lambda b, j: (b, j, 0))
    return pl.pallas_call(
        _branch_b_kernel,
        grid=(B, S // tm),
        in_specs=[pl.BlockSpec((1, tm, D), lambda b, j: (b, j, 0)),
                  _const_spec((1, D)), vec, vec, rope_spec, rope_spec,
                  _const_spec(w.shape), _const_spec(wg.shape), _const_spec(dmask.shape),
                  _const_spec(xi.shape), _const_spec(zeta.shape), _const_spec(g_chunk.shape),
                  _const_spec(rg.shape), _const_spec(wo.shape)],
        out_specs=pl.BlockSpec((1, tm, D), lambda b, j: (b, j, 0)),
        out_shape=jax.ShapeDtypeStruct((B, S, D), BF16),
        scratch_shapes=[pltpu.VMEM((tm, dr), BF16), pltpu.VMEM((tm, dr), F32),
                        pltpu.VMEM((tm, dr), BF16), pltpu.VMEM((tm, dr), F32),
                        pltpu.VMEM((RET_HEADS, RET_HEAD_DIM, RET_HEAD_DIM), F32)],
        compiler_params=_params(2),
        name="branch_b",
    )(x, g, sh, sc, cosf, sinf, w, wg, dmask, xi, zeta, g_chunk, rg, wo)


def _out_router_kernel(ma_ref, mb_ref, mc_ref, x_ref, g1_ref, wo_ref, n2_ref, sh_ref, sc_ref,
                       rw_ref, rb_ref, x1_ref, idx_ref, prob_ref, rank_ref, cnt_ref, *rest):
    h2_refs, carry_ref = rest[:MOE_SPLIT], rest[MOE_SPLIT]
    tm = x_ref.shape[1]
    first = jnp.logical_and(pl.program_id(0) == 0, pl.program_id(1) == 0)

    @pl.when(first)
    def _():
        carry_ref[...] = jnp.zeros(carry_ref.shape, F32)

    m = ma_ref[0].astype(F32) + mb_ref[0].astype(F32) + mc_ref[0].astype(F32)
    x1 = x_ref[0] + g1_ref[0] * _bdot(m, wo_ref[...])
    x1_ref[0] = x1
    h2 = _norm_mod(x1, n2_ref[...], sh_ref[0], sc_ref[0])
    for h2_ref, part in zip(h2_refs, _pack_rows(h2)):
        h2_ref[...] = part

    logits = jnp.dot(h2, rw_ref[...], preferred_element_type=F32,
                     precision=lax.Precision.HIGHEST) + rb_ref[...]
    lane = lax.broadcasted_iota(jnp.int32, logits.shape, 1)
    lane_f = lane.astype(F32)
    neg = jnp.float32(-jnp.inf)
    work = jnp.where(lane < N_EXPERTS, logits, neg)
    idx_out = jnp.zeros(logits.shape, jnp.int32)
    val_out = jnp.full(logits.shape, neg, F32)
    onehot_all = jnp.zeros(logits.shape, F32)
    sels = []
    for kk in range(TOP_K):
        mx = jnp.max(work, axis=-1, keepdims=True)
        first_idx = jnp.min(jnp.where(work == mx, lane_f, float(LANES)), axis=-1, keepdims=True)
        sel = lane_f == first_idx
        sels.append(sel)
        idx_out = jnp.where(lane == kk, first_idx.astype(jnp.int32), idx_out)
        val_out = jnp.where(lane == kk, mx, val_out)
        onehot_all = onehot_all + sel.astype(F32)
        work = jnp.where(sel, neg, work)

    ex = jnp.exp(val_out - val_out[:, 0:1])
    prob_ref[0] = ex / jnp.sum(ex, axis=-1, keepdims=True)
    idx_ref[0] = idx_out

    row = lax.broadcasted_iota(jnp.int32, (tm, tm), 0)
    col = lax.broadcasted_iota(jnp.int32, (tm, tm), 1)
    lower = (col < row).astype(BF16)
    before = jnp.dot(lower, onehot_all.astype(BF16), preferred_element_type=F32) + carry_ref[...]
    rank_out = jnp.zeros(logits.shape, jnp.int32)
    for kk in range(TOP_K):
        rk = jnp.sum(jnp.where(sels[kk], before, 0.0), axis=-1, keepdims=True)
        rank_out = jnp.where(lane == kk, rk.astype(jnp.int32), rank_out)
    rank_ref[0] = rank_out
    total = carry_ref[...] + jnp.sum(onehot_all, axis=0, keepdims=True)
    carry_ref[...] = total
    cnt_ref[...] = total


def _out_router(ma, mb, mc, x, g1, wo, n2, sh2, sc2, rw, rb):
    B, S, D = x.shape
    tm = min(ROW_TILE, S)
    tile = pl.BlockSpec((1, tm, D), lambda b, j: (b, j, 0))
    vec = pl.BlockSpec((1, 1, D), lambda b, j: (b, 0, 0))
    lanes = pl.BlockSpec((1, tm, LANES), lambda b, j: (b, j, 0))
    dq = D // (2 * MOE_SPLIT)
    per_seq = S // tm
    part = pl.BlockSpec((tm, dq), lambda b, j: (b * per_seq + j, 0))
    return pl.pallas_call(
        _out_router_kernel,
        grid=(B, S // tm),
        in_specs=[tile, tile, tile, tile, vec, _const_spec(wo.shape), _const_spec((1, D)),
                  vec, vec, _const_spec(rw.shape), _const_spec(rb.shape)],
        out_specs=[tile, lanes, lanes, lanes, pl.BlockSpec((1, LANES), lambda b, j: (0, 0))]
        + [part] * MOE_SPLIT,
        out_shape=[jax.ShapeDtypeStruct((B, S, D), F32),
                   jax.ShapeDtypeStruct((B, S, LANES), jnp.int32),
                   jax.ShapeDtypeStruct((B, S, LANES), F32),
                   jax.ShapeDtypeStruct((B, S, LANES), jnp.int32),
                   jax.ShapeDtypeStruct((1, LANES), F32)]
        + [jax.ShapeDtypeStruct((B * S, dq), jnp.uint32)] * MOE_SPLIT,
        scratch_shapes=[pltpu.VMEM((1, LANES), F32)],
        compiler_params=_params(2),
        name="out_router",
    )(ma, mb, mc, x, g1, wo, n2, sh2, sc2, rw, rb)


def _sc_row_gather(src, indices):
    n = indices.shape[0]
    d = src.shape[1]
    w = SC_GATHER_WINDOW
    mesh = plsc.VectorSubcoreMesh(core_axis_name="core", subcore_axis_name="subcore")
    per = n // w // SC_SUBCORES
    window = lambda i: (i % per) * SC_SUBCORES + i // per

    @pl.kernel(out_type=jax.ShapeDtypeStruct((n, d), src.dtype), mesh=mesh, scratch_types=[])
    def gather_kernel(src_hbm, idx_hbm, out_hbm):
        def body(idx_vmem, out_vmem):
            pltpu.sync_copy(src_hbm.at[idx_vmem.at[0]], out_vmem)

        pltpu.emit_pipeline(
            body,
            grid=(n // w,),
            in_specs=[pl.BlockSpec((1, w), index_map=lambda i: (0, window(i)))],
            out_specs=[pl.BlockSpec((w, d), index_map=lambda i: (window(i), 0))],
            core_axis_name=("core", "subcore"),
            dimension_semantics=(pltpu.PARALLEL,),
        )(idx_hbm, out_hbm)

    return gather_kernel(src, indices.reshape(1, n))


def _expert_kernel(be_ref, *refs):
    x_refs, refs = refs[:MOE_SPLIT], refs[MOE_SPLIT:]
    w1_ref, b1_ref, w2_ref, b2_ref = refs[:4]
    o_refs, (w1b_ref, w2b_ref) = refs[4:4 + MOE_SPLIT], refs[4 + MOE_SPLIT:]
    i = pl.program_id(0)
    f = w2_ref.shape[2]

    @pl.when(jnp.logical_or(i == 0, be_ref[i] != be_ref[jnp.maximum(i - 1, 0)]))
    def _():
        w1b_ref[...] = w1_ref[0, 0].astype(BF16)
        w2b_ref[...] = w2_ref[0, 0].astype(BF16)

    xb = _unpack_rows([x_ref[...] for x_ref in x_refs]).astype(BF16)
    gu = jnp.dot(xb, w1b_ref[...], preferred_element_type=F32) + b1_ref[0, 0]
    glu = jnp.minimum(gu[:, :f], SWIGLU_LIMIT)
    lin = jnp.clip(gu[:, f:], -SWIGLU_LIMIT, SWIGLU_LIMIT)
    act = glu * _sigmoid(SWIGLU_ALPHA * glu) * (lin + 1.0)
    y = _bdot(act, w2b_ref[...]) + b2_ref[0, 0]
    for o_ref, part in zip(o_refs, _pack_rows(y)):
        o_ref[...] = part


def _experts(xparts, blk_expert, w1, b1, w2, b2, layer):
    rows, dq = xparts[0].shape
    L, E, D, F2 = w1.shape
    nb = blk_expert.shape[0]
    blk = EXPERT_BLOCK
    per_expert = lambda shape: pl.BlockSpec((1, 1) + shape, lambda i, be: (layer, be[i], 0, 0))
    grid_spec = pltpu.PrefetchScalarGridSpec(
        num_scalar_prefetch=1,
        grid=(nb,),
        in_specs=[pl.BlockSpec((blk, dq), lambda i, be: (i, 0))] * MOE_SPLIT
        + [per_expert((D, F2)), per_expert((1, F2)), per_expert((F2 // 2, D)), per_expert((1, D))],
        out_specs=[pl.BlockSpec((blk, dq), lambda i, be: (i, 0))] * MOE_SPLIT,
        scratch_shapes=[pltpu.VMEM((D, F2), BF16), pltpu.VMEM((F2 // 2, D), BF16)])
    return pl.pallas_call(
        _expert_kernel,
        grid_spec=grid_spec,
        out_shape=[jax.ShapeDtypeStruct((rows, dq), jnp.uint32)] * MOE_SPLIT,
        compiler_params=_params(1),
        name="experts",
    )(blk_expert, *xparts, w1, b1.reshape(L, E, 1, F2), w2, b2.reshape(L, E, 1, D))


def _combine_kernel(*refs, final_norm):
    y_refs, (x1_ref, g2_ref, prob_ref, fg_ref, o_ref) = refs[:MOE_SPLIT], refs[MOE_SPLIT:]
    tm = x1_ref.shape[0]
    prob = prob_ref[...]
    acc = jnp.zeros(x1_ref.shape, F32)
    for kk in range(TOP_K):
        rows = slice(kk * tm, (kk + 1) * tm)
        acc = acc + prob[:, kk:kk + 1] * _unpack_rows([y_ref[rows, :] for y_ref in y_refs])
    x2 = x1_ref[...] + g2_ref[0] * acc
    if final_norm:
        x2 = x2 * lax.rsqrt(jnp.mean(x2 * x2, axis=-1, keepdims=True) + EPS) * fg_ref[...]
    o_ref[...] = x2


def _combine(yparts, x1, g2, prob, final_g, seq_len, final_norm):
    N, D = x1.shape
    tm = min(COMBINE_TILE, seq_len)
    per_seq = seq_len // tm
    return pl.pallas_call(
        functools.partial(_combine_kernel, final_norm=final_norm),
        grid=(N // tm,),
        in_specs=[pl.BlockSpec((TOP_K * tm, D // (2 * MOE_SPLIT)), lambda i: (i, 0))] * MOE_SPLIT
        + [pl.BlockSpec((tm, D), lambda i: (i, 0)),
                  pl.BlockSpec((1, 1, D), lambda i: (i // per_seq, 0, 0)),
                  pl.BlockSpec((tm, LANES), lambda i: (i, 0)),
                  pl.BlockSpec((1, D), lambda i: (0, 0))],
        out_specs=pl.BlockSpec((tm, D), lambda i: (i, 0)),
        out_shape=jax.ShapeDtypeStruct((N, D), F32),
        compiler_params=_params(1),
        name="combine",
    )(*yparts, x1, g2, prob, final_g)


def _routing_tables(counts, idx, rank, n_tokens):
    blk = EXPERT_BLOCK
    A = n_tokens * TOP_K
    nb = (A + N_EXPERTS * (blk - 1)) // blk
    group = max(1, SC_GATHER_ROWS // blk)
    nb = -(-nb // group) * group
    counts = counts.astype(jnp.int32)
    padded = ((counts + blk - 1) // blk) * blk
    pends = jnp.cumsum(padded)
    pstarts = pends - padded
    block_start = jnp.arange(nb, dtype=jnp.int32) * blk
    blk_expert = jnp.minimum(jnp.sum(block_start[:, None] >= pends[None, :], axis=1),
                             N_EXPERTS - 1).astype(jnp.int32)
    dest = pstarts[idx] + rank
    tok = jnp.repeat(jnp.arange(n_tokens, dtype=jnp.int32), TOP_K)
    n_pad = nb * blk - A
    pad_per = padded - counts
    padcum = jnp.cumsum(pad_per)
    q = jnp.arange(n_pad, dtype=jnp.int32)
    e_q = jnp.sum(q[:, None] >= padcum[None, :], axis=1).astype(jnp.int32)
    e_c = jnp.minimum(e_q, N_EXPERTS - 1)
    key_in = (pstarts + counts - (padcum - pad_per))[e_c] + q
    key_tail = pends[-1] - padcum[-1] + q
    pad_keys = jnp.where(e_q < N_EXPERTS, key_in, key_tail)
    keys = jnp.concatenate([dest.reshape(A), pad_keys])
    vals = jnp.concatenate([tok, jnp.zeros((n_pad,), jnp.int32)])
    _, tok_of_slot = lax.sort((keys, vals), num_keys=1)
    return dest, tok_of_slot, blk_expert


def kernel(x, c, positions, mod_w, mod_b, norm1_g, w_in, conv_a_w, w_a_out, ret_norm_g, w_b_out, glu_b, conv_c_w, conv_c_b, ln_c_g, ln_c_b, w_c_out, w_o, norm2_g, router_w, router_b, exp_w1, exp_b1, exp_w2, exp_b2, final_g):
    B, S, D = x.shape
    L = mod_w.shape[0]
    N = B * S
    da = w_a_out.shape[1]
    dr = w_b_out.shape[1]
    dc = w_c_out.shape[1]
    s1 = 3 * da
    s2 = s1 + 4 * dr
    s3 = s2 + 2 * dc

    mod = _modulation(c, mod_w, mod_b)
    cosf, sinf = _rope_tables(positions)
    tables = _retention_tables()
    row = lambda v: v.reshape(1, -1)
    rw_pad = jnp.zeros((L, D, LANES), F32).at[:, :, :N_EXPERTS].set(router_w)
    rb_pad = jnp.zeros((L, 1, LANES), F32).at[:, 0, :N_EXPERTS].set(router_b)

    for l in range(L):
        sh1, sc1, g1, sh2, sc2, g2 = [mod[l, :, i * D:(i + 1) * D].reshape(B, 1, D) for i in range(6)]
        wl = w_in[l].astype(BF16)
        wg = wl[:, s3:]
        n1 = row(norm1_g[l])
        ma = _branch_a(x, n1, sh1, sc1, wl[:, :s1], wg[:, :D], conv_a_w[l],
                       w_a_out[l].astype(BF16))
        mb = _branch_b(x, n1, sh1, sc1, cosf, sinf, wl[:, s1:s2], wg[:, D:2 * D], tables,
                       row(ret_norm_g[l]), w_b_out[l].astype(BF16))
        mc = _branch_c(x, n1, sh1, sc1, wl[:, s2:s3], wg[:, 2 * D:], row(glu_b[l]), conv_c_w[l],
                       row(conv_c_b[l]), row(ln_c_g[l]), row(ln_c_b[l]), w_c_out[l].astype(BF16))
        x1, idx, prob, rank, counts, *h2_parts = _out_router(
            ma, mb, mc, x, g1, w_o[l].astype(BF16), row(norm2_g[l]), sh2, sc2, rw_pad[l], rb_pad[l])
        idx = idx.reshape(N, LANES)[:, :TOP_K]
        rank = rank.reshape(N, LANES)[:, :TOP_K]
        dest, tok_of_slot, blk_expert = _routing_tables(counts[0, :N_EXPERTS], idx, rank, N)
        xparts = [_sc_row_gather(p, tok_of_slot) for p in h2_parts]
        yparts = _experts(xparts, blk_expert, exp_w1, exp_b1, exp_w2, exp_b2, layer=l)
        tmc = min(COMBINE_TILE, S)
        dest_kmajor = dest.reshape(N // tmc, tmc, TOP_K).transpose(0, 2, 1).reshape(N * TOP_K)
        ycat = [_sc_row_gather(p, dest_kmajor) for p in yparts]
        x = _combine(ycat, x1.reshape(N, D), g2, prob.reshape(N, LANES), row(final_g),
                     S, final_norm=(l == L - 1)).reshape(B, S, D)
    return x
```

```python
import functools

import jax
import jax.numpy as jnp
from jax import lax
from jax.experimental import pallas as pl
from jax.experimental.pallas import tpu as pltpu
from jax.experimental.pallas import tpu_sc as plsc

F32 = jnp.float32
BF16 = jnp.bfloat16

EPS = 1e-6
SHORT_CONV_W = 3
RET_HEADS = 8
RET_HEAD_DIM = 128
RET_CHUNK = 128
ROPE_BASE = 10000.0
CONF_KERNEL = 31
N_EXPERTS = 32
TOP_K = 4
SWIGLU_LIMIT = 7.0
SWIGLU_ALPHA = 1.702

LANES = 128
SUBLANES = 8
VMEM_LIMIT = 56 * 1024 * 1024

ROW_TILE = 512
EXPERT_BLOCK = 512
SC_GATHER_WINDOW = 128
MOE_SPLIT = 2
SC_SUBCORES = 32
SC_GATHER_ROWS = SC_GATHER_WINDOW * SC_SUBCORES
COMBINE_TILE = 256
CONV_ROWS = 32
CONF_HALO = 32


def _params(n_grid):
    return pltpu.CompilerParams(
        dimension_semantics=("arbitrary",) * n_grid, vmem_limit_bytes=VMEM_LIMIT)


def _const_spec(shape):
    nd = len(shape)
    return pl.BlockSpec(shape, lambda *_: (0,) * nd, pipeline_mode=pl.Buffered(1))


def _sigmoid(x):
    return 1.0 / (1.0 + jnp.exp(-x))


def _silu(x):
    return x * _sigmoid(x)


def _norm_mod(x, g, sh, sc):
    y = x * lax.rsqrt(jnp.mean(x * x, axis=-1, keepdims=True) + EPS)
    return (y * g) * (1.0 + sc) + sh


def _bdot(a, b):
    return jnp.dot(a.astype(BF16), b.astype(BF16), preferred_element_type=F32)


def _pack_rows(v):
    c = v.shape[1] // MOE_SPLIT
    parts = []
    for p in range(MOE_SPLIT):
        lo = v[:, p * c:p * c + c // 2].astype(BF16).astype(F32)
        hi = v[:, p * c + c // 2:(p + 1) * c].astype(BF16).astype(F32)
        lo_bits = lax.bitcast_convert_type(lo, jnp.uint32) >> 16
        hi_bits = lax.bitcast_convert_type(hi, jnp.uint32) & jnp.uint32(0xFFFF0000)
        parts.append(hi_bits | lo_bits)
    return parts


def _unpack_rows(parts):
    cols = []
    for w in parts:
        cols.append(lax.bitcast_convert_type(w << 16, F32))
        cols.append(lax.bitcast_convert_type(w & jnp.uint32(0xFFFF0000), F32))
    return jnp.concatenate(cols, axis=1)


def _mod_kernel(c_ref, w_ref, b_ref, o_ref):
    cond = _silu(c_ref[...])
    o_ref[0] = jnp.dot(cond, w_ref[0], preferred_element_type=F32,
                       precision=lax.Precision.HIGHEST) + b_ref[0]


def _modulation(c, mod_w, mod_b):
    L, D, M = mod_w.shape
    B = c.shape[0]
    tn = 1024
    return pl.pallas_call(
        _mod_kernel,
        grid=(L, M // tn),
        in_specs=[pl.BlockSpec((B, D), lambda l, j: (0, 0)),
                  pl.BlockSpec((1, D, tn), lambda l, j: (l, 0, j)),
                  pl.BlockSpec((1, 1, tn), lambda l, j: (l, 0, j))],
        out_specs=pl.BlockSpec((1, B, tn), lambda l, j: (l, 0, j)),
        out_shape=jax.ShapeDtypeStruct((L, B, M), F32),
        compiler_params=_params(2),
        name="modulation",
    )(c, mod_w, mod_b.reshape(L, 1, M))


def _rope_kernel(pos_ref, invf_ref, cos_ref, sin_ref):
    ang = pos_ref[0].astype(F32) * invf_ref[...]
    lane = lax.broadcasted_iota(jnp.int32, ang.shape, 1)
    s = jnp.sin(ang)
    cos_ref[0] = jnp.cos(ang)
    sin_ref[0] = jnp.where(lane < RET_HEAD_DIM // 2, -s, s)


def _rope_tables(positions):
    B, S = positions.shape
    inv_freq = 1.0 / (ROPE_BASE ** (jnp.arange(0, RET_HEAD_DIM, 2, dtype=F32) / RET_HEAD_DIM))
    invf = jnp.concatenate([inv_freq, inv_freq]).reshape(1, RET_HEAD_DIM)
    ts = min(S, 1024)
    spec = pl.BlockSpec((1, ts, RET_HEAD_DIM), lambda b, j: (b, j, 0))
    return pl.pallas_call(
        _rope_kernel,
        grid=(B, S // ts),
        in_specs=[pl.BlockSpec((1, ts, 1), lambda b, j: (b, j, 0)),
                  pl.BlockSpec((1, RET_HEAD_DIM), lambda b, j: (0, 0))],
        out_specs=[spec, spec],
        out_shape=[jax.ShapeDtypeStruct((B, S, RET_HEAD_DIM), F32)] * 2,
        compiler_params=_params(2),
        name="rope_tables",
    )(positions.reshape(B, S, 1), invf)


def _branch_a_kernel(x_ref, g_ref, sh_ref, sc_ref, w_ref, wg_ref, cw_ref, wo_ref, o_ref, buf_ref):
    tm = x_ref.shape[1]
    da = wo_ref.shape[0]

    @pl.when(pl.program_id(1) == 0)
    def _():
        buf_ref[0:SUBLANES, :] = jnp.zeros((SUBLANES, da), F32)

    h = _norm_mod(x_ref[0], g_ref[...], sh_ref[0], sc_ref[0]).astype(BF16)
    p = jnp.dot(h, w_ref[...], preferred_element_type=F32)
    b_g, c_g, xa = p[:, :da], p[:, da:2 * da], p[:, 2 * da:]
    u = c_g * xa
    buf_ref[SUBLANES:SUBLANES + tm, :] = u
    conv = cw_ref[2:3, :] * u
    for j in range(SHORT_CONV_W - 1):
        off = SUBLANES - (SHORT_CONV_W - 1) + j
        conv = conv + cw_ref[j:j + 1, :] * buf_ref[off:off + tm, :]
    buf_ref[0:SUBLANES, :] = buf_ref[tm:tm + SUBLANES, :]
    y_a = _bdot(b_g * conv, wo_ref[...])
    gate = _sigmoid(jnp.dot(h, wg_ref[...], preferred_element_type=F32))
    o_ref[0] = (gate * y_a).astype(BF16)


def _branch_a(x, g, sh, sc, w, wg, cw, wo):
    B, S, D = x.shape
    da = wo.shape[0]
    tm = min(ROW_TILE, S)
    vec = pl.BlockSpec((1, 1, D), lambda b, j: (b, 0, 0))
    return pl.pallas_call(
        _branch_a_kernel,
        grid=(B, S // tm),
        in_specs=[pl.BlockSpec((1, tm, D), lambda b, j: (b, j, 0)),
                  _const_spec((1, D)), vec, vec,
                  _const_spec(w.shape), _const_spec(wg.shape), _const_spec(cw.shape),
                  _const_spec(wo.shape)],
        out_specs=pl.BlockSpec((1, tm, D), lambda b, j: (b, j, 0)),
        out_shape=jax.ShapeDtypeStruct((B, S, D), BF16),
        scratch_shapes=[pltpu.VMEM((tm + SUBLANES, da), F32)],
        compiler_params=_params(2),
        name="branch_a",
    )(x, g, sh, sc, w, wg, cw, wo)


def _branch_c_kernel(x_ref, g_ref, sh_ref, sc_ref, w_ref, wg_ref, glub_ref, cw_ref, cb_ref,
                     lng_ref, lnb_ref, wo_ref, o_ref, buf_ref, conv_ref):
    tm = x_ref.shape[1]
    dc = wo_ref.shape[0]

    @pl.when(pl.program_id(1) == 0)
    def _():
        buf_ref[0, 0:CONF_HALO, :] = jnp.zeros((CONF_HALO, dc), F32)

    h = _norm_mod(x_ref[0], g_ref[...], sh_ref[0], sc_ref[0]).astype(BF16)
    p = jnp.dot(h, w_ref[...], preferred_element_type=F32) + glub_ref[...]
    buf_ref[0, CONF_HALO:CONF_HALO + tm, :] = p[:, :dc] * _sigmoid(p[:, dc:])
    for b in range(1, SUBLANES):
        buf_ref[b, SUBLANES:, :] = buf_ref[0, SUBLANES - b:tm + CONF_HALO - b, :]

    base = CONF_HALO - (CONF_KERNEL - 1)

    def strip(s, carry):
        r0 = pl.multiple_of(s * CONV_ROWS, CONV_ROWS)
        acc = jnp.zeros((CONV_ROWS, dc), F32)
        for j in range(CONF_KERNEL):
            a = -(-(base + j) // SUBLANES)
            b = a * SUBLANES - (base + j)
            acc = acc + cw_ref[j:j + 1, :] * buf_ref[b, pl.ds(r0 + a * SUBLANES, CONV_ROWS), :]
        conv_ref[pl.ds(r0, CONV_ROWS), :] = acc
        return carry

    lax.fori_loop(0, tm // CONV_ROWS, strip, 0)
    buf_ref[0, 0:CONF_HALO, :] = buf_ref[0, tm:tm + CONF_HALO, :]

    u = conv_ref[...] + cb_ref[...]
    mu = jnp.mean(u, axis=-1, keepdims=True)
    d = u - mu
    var = jnp.mean(d * d, axis=-1, keepdims=True)
    y = d * lax.rsqrt(var + EPS) * lng_ref[...] + lnb_ref[...]
    y_c = _bdot(_silu(y), wo_ref[...])
    gate = _sigmoid(jnp.dot(h, wg_ref[...], preferred_element_type=F32))
    o_ref[0] = (gate * y_c).astype(BF16)


def _branch_c(x, g, sh, sc, w, wg, glub, cw, cb, lng, lnb, wo):
    B, S, D = x.shape
    dc = wo.shape[0]
    tm = min(ROW_TILE, S)
    vec = pl.BlockSpec((1, 1, D), lambda b, j: (b, 0, 0))
    return pl.pallas_call(
        _branch_c_kernel,
        grid=(B, S // tm),
        in_specs=[pl.BlockSpec((1, tm, D), lambda b, j: (b, j, 0)),
                  _const_spec((1, D)), vec, vec,
                  _const_spec(w.shape), _const_spec(wg.shape), _const_spec(glub.shape),
                  _const_spec(cw.shape), _const_spec(cb.shape), _const_spec(lng.shape),
                  _const_spec(lnb.shape), _const_spec(wo.shape)],
        out_specs=pl.BlockSpec((1, tm, D), lambda b, j: (b, j, 0)),
        out_shape=jax.ShapeDtypeStruct((B, S, D), BF16),
        scratch_shapes=[pltpu.VMEM((SUBLANES, tm + CONF_HALO, dc), F32), pltpu.VMEM((tm, dc), F32)],
        compiler_params=_params(2),
        name="branch_c",
    )(x, g, sh, sc, w, wg, glub, cw, cb, lng, lnb, wo)


def _branch_b_kernel(x_ref, g_ref, sh_ref, sc_ref, cos_ref, sin_ref, w_ref, wg_ref,
                     dmask_ref, xi_ref, zeta_ref, gch_ref, rg_ref, wo_ref, o_ref,
                     q_ref, k_ref, v_ref, r_ref, state_ref):
    tm = x_ref.shape[1]
    dr = wo_ref.shape[0]
    hd = RET_HEAD_DIM

    @pl.when(pl.program_id(1) == 0)
    def _():
        state_ref[...] = jnp.zeros(state_ref.shape, F32)

    h = _norm_mod(x_ref[0], g_ref[...], sh_ref[0], sc_ref[0]).astype(BF16)
    cos = cos_ref[0]
    sin = sin_ref[0]
    scale = hd ** -0.5

    def rope(t):
        return t * cos + pltpu.roll(t, hd // 2, axis=1) * sin

    q = jnp.dot(h, w_ref[:, 0:dr], preferred_element_type=F32) * scale
    k = jnp.dot(h, w_ref[:, dr:2 * dr], preferred_element_type=F32)
    for hh in range(RET_HEADS):
        sl = slice(hh * hd, (hh + 1) * hd)
        q_ref[:, sl] = rope(q[:, sl]).astype(BF16)
        k_ref[:, sl] = rope(k[:, sl])
    v_ref[...] = jnp.dot(h, w_ref[:, 2 * dr:3 * dr], preferred_element_type=F32).astype(BF16)

    def chunk(ci, carry):
        r0 = pl.multiple_of(ci * RET_CHUNK, RET_CHUNK)
        rows = pl.ds(r0, RET_CHUNK)
        for hh in range(RET_HEADS):
            sl = slice(hh * hd, (hh + 1) * hd)
            qh = q_ref[rows, sl]
            kh = k_ref[rows, sl]
            vh = v_ref[rows, sl]
            st = state_ref[hh]
            scores = lax.dot_general(qh, kh.astype(BF16), (((1,), (1,)), ((), ())),
                                     preferred_element_type=F32) * dmask_ref[hh]
            inner = _bdot(scores, vh)
            cross = _bdot(qh, st) * xi_ref[hh]
            kz = (kh * zeta_ref[hh]).astype(BF16)
            state_ref[hh] = st * gch_ref[hh] + lax.dot_general(
                kz, vh, (((0,), (0,)), ((), ())), preferred_element_type=F32)
            o = inner + cross
            mu = jnp.mean(o, axis=-1, keepdims=True)
            d = o - mu
            var = jnp.mean(d * d, axis=-1, keepdims=True)
            r_ref[rows, sl] = d * lax.rsqrt(var + EPS)
        return carry

    lax.fori_loop(0, tm // RET_CHUNK, chunk, 0)

    og = jnp.dot(h, w_ref[:, 3 * dr:4 * dr], preferred_element_type=F32)
    y_b = _bdot(_silu(og) * (r_ref[...] * rg_ref[...]), wo_ref[...])
    gate = _sigmoid(jnp.dot(h, wg_ref[...], preferred_element_type=F32))
    o_ref[0] = (gate * y_b).astype(BF16)


def _retention_tables():
    H, C = RET_HEADS, RET_CHUNK
    gamma = 1.0 - jnp.power(2.0, -5.0 - jnp.arange(H, dtype=F32))
    lg = jnp.log(gamma)
    idx = jnp.arange(C, dtype=F32)
    rel = idx[:, None] - idx[None, :]
    dmask = jnp.where(rel[None] >= 0, jnp.exp(jnp.maximum(rel, 0.0)[None] * lg[:, None, None]), 0.0)
    xi = jnp.exp((idx + 1.0)[None] * lg[:, None])
    zeta = jnp.exp((C - 1.0 - idx)[None] * lg[:, None])
    g_chunk = jnp.exp(C * lg)
    bc = lambda t: jnp.broadcast_to(t[:, :, None], (H, C, RET_HEAD_DIM)).astype(F32)
    g_rows = jnp.broadcast_to(g_chunk[:, None, None], (H, 1, RET_HEAD_DIM)).astype(F32)
    return dmask.astype(F32), bc(xi), bc(zeta), g_rows


def _branch_b(x, g, sh, sc, cosf, sinf, w, wg, tables, rg, wo):
    B, S, D = x.shape
    dr = wo.shape[0]
    tm = min(ROW_TILE, S)
    dmask, xi, zeta, g_chunk = tables
    vec = pl.BlockSpec((1, 1, D), lambda b, j: (b, 0, 0))
    rope_spec = pl.BlockSpec((1, tm, RET_HEAD_DIM), lambda b, j: (b, j, 0))
    return pl.pallas_call(
        _branch_b_kernel,
        grid=(B, S // tm),
        in_specs=[pl.BlockSpec((1, tm, D), lambda b, j: (b, j, 0)),
                  _const_spec((1, D)), vec, vec, rope_spec, rope_spec,
                  _const_spec(w.shape), _const_spec(wg.shape), _const_spec(dmask.shape),
                  _const_spec(xi.shape), _const_spec(zeta.shape), _const_spec(g_chunk.shape),
                  _const_spec(rg.shape), _const_spec(wo.shape)],
        out_specs=pl.BlockSpec((1, tm, D), lambda b, j: (b, j, 0)),
        out_shape=jax.ShapeDtypeStruct((B, S, D), BF16),
        scratch_shapes=[pltpu.VMEM((tm, dr), BF16), pltpu.VMEM((tm, dr), F32),
                        pltpu.VMEM((tm, dr), BF16), pltpu.VMEM((tm, dr), F32),
                        pltpu.VMEM((RET_HEADS, RET_HEAD_DIM, RET_HEAD_DIM), F32)],
        compiler_params=_params(2),
        name="branch_b",
    )(x, g, sh, sc, cosf, sinf, w, wg, dmask, xi, zeta, g_chunk, rg, wo)


def _out_router_kernel(ma_ref, mb_ref, mc_ref, x_ref, g1_ref, wo_ref, n2_ref, sh_ref, sc_ref,
                       rw_ref, rb_ref, x1_ref, idx_ref, prob_ref, rank_ref, cnt_ref, *rest):
    h2_refs, carry_ref = rest[:MOE_SPLIT], rest[MOE_SPLIT]
    tm = x_ref.shape[1]
    first = jnp.logical_and(pl.program_id(0) == 0, pl.program_id(1) == 0)

    @pl.when(first)
    def _():
        carry_ref[...] = jnp.zeros(carry_ref.shape, F32)

    m = ma_ref[0].astype(F32) + mb_ref[0].astype(F32) + mc_ref[0].astype(F32)
    x1 = x_ref[0] + g1_ref[0] * _bdot(m, wo_ref[...])
    x1_ref[0] = x1
    h2 = _norm_mod(x1, n2_ref[...], sh_ref[0], sc_ref[0])
    for h2_ref, part in zip(h2_refs, _pack_rows(h2)):
        h2_ref[...] = part

    logits = jnp.dot(h2, rw_ref[...], preferred_element_type=F32,
                     precision=lax.Precision.HIGHEST) + rb_ref[...]
    lane = lax.broadcasted_iota(jnp.int32, logits.shape, 1)
    lane_f = lane.astype(F32)
    neg = jnp.float32(-jnp.inf)
    work = jnp.where(lane < N_EXPERTS, logits, neg)
    idx_out = jnp.zeros(logits.shape, jnp.int32)
    val_out = jnp.full(logits.shape, neg, F32)
    onehot_all = jnp.zeros(logits.shape, F32)
    sels = []
    for kk in range(TOP_K):
        mx = jnp.max(work, axis=-1, keepdims=True)
        first_idx = jnp.min(jnp.where(work == mx, lane_f, float(LANES)), axis=-1, keepdims=True)
        sel = lane_f == first_idx
        sels.append(sel)
        idx_out = jnp.where(lane == kk, first_idx.astype(jnp.int32), idx_out)
        val_out = jnp.where(lane == kk, mx, val_out)
        onehot_all = onehot_all + sel.astype(F32)
        work = jnp.where(sel, neg, work)

    ex = jnp.exp(val_out - val_out[:, 0:1])
    prob_ref[0] = ex / jnp.sum(ex, axis=-1, keepdims=True)
    idx_ref[0] = idx_out

    row = lax.broadcasted_iota(jnp.int32, (tm, tm), 0)
    col = lax.broadcasted_iota(jnp.int32, (tm, tm), 1)
    lower = (col < row).astype(BF16)
    before = jnp.dot(lower, onehot_all.astype(BF16), preferred_element_type=F32) + carry_ref[...]
    rank_out = jnp.zeros(logits.shape, jnp.int32)
    for kk in range(TOP_K):
        rk = jnp.sum(jnp.where(sels[kk], before, 0.0), axis=-1, keepdims=True)
        rank_out = jnp.where(lane == kk, rk.astype(jnp.int32), rank_out)
    rank_ref[0] = rank_out
    total = carry_ref[...] + jnp.sum(onehot_all, axis=0, keepdims=True)
    carry_ref[...] = total
    cnt_ref[...] = total


def _out_router(ma, mb, mc, x, g1, wo, n2, sh2, sc2, rw, rb):
    B, S, D = x.shape
    tm = min(ROW_TILE, S)
    tile = pl.BlockSpec((1, tm, D), lambda b, j: (b, j, 0))
    vec = pl.BlockSpec((1, 1, D), lambda b, j: (b, 0, 0))
    lanes = pl.BlockSpec((1, tm, LANES), lambda b, j: (b, j, 0))
    dq = D // (2 * MOE_SPLIT)
    per_seq = S // tm
    part = pl.BlockSpec((tm, dq), lambda b, j: (b * per_seq + j, 0))
    return pl.pallas_call(
        _out_router_kernel,
        grid=(B, S // tm),
        in_specs=[tile, tile, tile, tile, vec, _const_spec(wo.shape), _const_spec((1, D)),
                  vec, vec, _const_spec(rw.shape), _const_spec(rb.shape)],
        out_specs=[tile, lanes, lanes, lanes, pl.BlockSpec((1, LANES), lambda b, j: (0, 0))]
        + [part] * MOE_SPLIT,
        out_shape=[jax.ShapeDtypeStruct((B, S, D), F32),
                   jax.ShapeDtypeStruct((B, S, LANES), jnp.int32),
                   jax.ShapeDtypeStruct((B, S, LANES), F32),
                   jax.ShapeDtypeStruct((B, S, LANES), jnp.int32),
                   jax.ShapeDtypeStruct((1, LANES), F32)]
        + [jax.ShapeDtypeStruct((B * S, dq), jnp.uint32)] * MOE_SPLIT,
        scratch_shapes=[pltpu.VMEM((1, LANES), F32)],
        compiler_params=_params(2),
        name="out_router",
    )(ma, mb, mc, x, g1, wo, n2, sh2, sc2, rw, rb)


def _sc_row_gather(src, indices):
    n = indices.shape[0]
    d = src.shape[1]
    w = SC_GATHER_WINDOW
    mesh = plsc.VectorSubcoreMesh(core_axis_name="core", subcore_axis_name="subcore")
    per = n // w // SC_SUBCORES
    window = lambda i: (i % per) * SC_SUBCORES + i // per

    @pl.kernel(out_type=jax.ShapeDtypeStruct((n, d), src.dtype), mesh=mesh, scratch_types=[])
    def gather_kernel(src_hbm, idx_hbm, out_hbm):
        def body(idx_vmem, out_vmem):
            pltpu.sync_copy(src_hbm.at[idx_vmem.at[0]], out_vmem)

        pltpu.emit_pipeline(
            body,
            grid=(n // w,),
            in_specs=[pl.BlockSpec((1, w), index_map=lambda i: (0, window(i)))],
            out_specs=[pl.BlockSpec((w, d), index_map=lambda i: (window(i), 0))],
            core_axis_name=("core", "subcore"),
            dimension_semantics=(pltpu.PARALLEL,),
        )(idx_hbm, out_hbm)

    return gather_kernel(src, indices.reshape(1, n))


def _expert_kernel(be_ref, *refs):
    x_refs, refs = refs[:MOE_SPLIT], refs[MOE_SPLIT:]
    w1_ref, b1_ref, w2_ref, b2_ref = refs[:4]
    o_refs, (w1b_ref, w2b_ref) = refs[4:4 + MOE_SPLIT], refs[4 + MOE_SPLIT:]
    i = pl.program_id(0)
    f = w2_ref.shape[2]

    @pl.when(jnp.logical_or(i == 0, be_ref[i] != be_ref[jnp.maximum(i - 1, 0)]))
    def _():
        w1b_ref[...] = w1_ref[0, 0].astype(BF16)
        w2b_ref[...] = w2_ref[0, 0].astype(BF16)

    xb = _unpack_rows([x_ref[...] for x_ref in x_refs]).astype(BF16)
    gu = jnp.dot(xb, w1b_ref[...], preferred_element_type=F32) + b1_ref[0, 0]
    glu = jnp.minimum(gu[:, :f], SWIGLU_LIMIT)
    lin = jnp.clip(gu[:, f:], -SWIGLU_LIMIT, SWIGLU_LIMIT)
    act = glu * _sigmoid(SWIGLU_ALPHA * glu) * (lin + 1.0)
    y = _bdot(act, w2b_ref[...]) + b2_ref[0, 0]
    for o_ref, part in zip(o_refs, _pack_rows(y)):
        o_ref[...] = part


def _experts(xparts, blk_expert, w1, b1, w2, b2, layer):
    rows, dq = xparts[0].shape
    L, E, D, F2 = w1.shape
    nb = blk_expert.shape[0]
    blk = EXPERT_BLOCK
    per_expert = lambda shape: pl.BlockSpec((1, 1) + shape, lambda i, be: (layer, be[i], 0, 0))
    grid_spec = pltpu.PrefetchScalarGridSpec(
        num_scalar_prefetch=1,
        grid=(nb,),
        in_specs=[pl.BlockSpec((blk, dq), lambda i, be: (i, 0))] * MOE_SPLIT
        + [per_expert((D, F2)), per_expert((1, F2)), per_expert((F2 // 2, D)), per_expert((1, D))],
        out_specs=[pl.BlockSpec((blk, dq), lambda i, be: (i, 0))] * MOE_SPLIT,
        scratch_shapes=[pltpu.VMEM((D, F2), BF16), pltpu.VMEM((F2 // 2, D), BF16)])
    return pl.pallas_call(
        _expert_kernel,
        grid_spec=grid_spec,
        out_shape=[jax.ShapeDtypeStruct((rows, dq), jnp.uint32)] * MOE_SPLIT,
        compiler_params=_params(1),
        name="experts",
    )(blk_expert, *xparts, w1, b1.reshape(L, E, 1, F2), w2, b2.reshape(L, E, 1, D))


def _combine_kernel(*refs, final_norm):
    y_refs, (x1_ref, g2_ref, prob_ref, fg_ref, o_ref) = refs[:MOE_SPLIT], refs[MOE_SPLIT:]
    tm = x1_ref.shape[0]
    prob = prob_ref[...]
    acc = jnp.zeros(x1_ref.shape, F32)
    for kk in range(TOP_K):
        rows = slice(kk * tm, (kk + 1) * tm)
        acc = acc + prob[:, kk:kk + 1] * _unpack_rows([y_ref[rows, :] for y_ref in y_refs])
    x2 = x1_ref[...] + g2_ref[0] * acc
    if final_norm:
        x2 = x2 * lax.rsqrt(jnp.mean(x2 * x2, axis=-1, keepdims=True) + EPS) * fg_ref[...]
    o_ref[...] = x2


def _combine(yparts, x1, g2, prob, final_g, seq_len, final_norm):
    N, D = x1.shape
    tm = min(COMBINE_TILE, seq_len)
    per_seq = seq_len // tm
    return pl.pallas_call(
        functools.partial(_combine_kernel, final_norm=final_norm),
        grid=(N // tm,),
        in_specs=[pl.BlockSpec((TOP_K * tm, D // (2 * MOE_SPLIT)), lambda i: (i, 0))] * MOE_SPLIT
        + [pl.BlockSpec((tm, D), lambda i: (i, 0)),
                  pl.BlockSpec((1, 1, D), lambda i: (i // per_seq, 0, 0)),
                  pl.BlockSpec((tm, LANES), lambda i: (i, 0)),
                  pl.BlockSpec((1, D), lambda i: (0, 0))],
        out_specs=pl.BlockSpec((tm, D), lambda i: (i, 0)),
        out_shape=jax.ShapeDtypeStruct((N, D), F32),
        compiler_params=_params(1),
        name="combine",
    )(*yparts, x1, g2, prob, final_g)


def _routing_tables(counts, idx, rank, n_tokens):
    blk = EXPERT_BLOCK
    A = n_tokens * TOP_K
    nb = (A + N_EXPERTS * (blk - 1)) // blk
    group = max(1, SC_GATHER_ROWS // blk)
    nb = -(-nb // group) * group
    counts = counts.astype(jnp.int32)
    padded = ((counts + blk - 1) // blk) * blk
    pends = jnp.cumsum(padded)
    pstarts = pends - padded
    block_start = jnp.arange(nb, dtype=jnp.int32) * blk
    blk_expert = jnp.minimum(jnp.sum(block_start[:, None] >= pends[None, :], axis=1),
                             N_EXPERTS - 1).astype(jnp.int32)
    nwin = padded // SC_GATHER_WINDOW
    def offset(r, e):
        nw = jnp.maximum(nwin[e], 1)
        return (r % nw) * SC_GATHER_WINDOW + r // nw
    dest = pstarts[idx] + offset(rank, idx)
    tok = jnp.repeat(jnp.arange(n_tokens, dtype=jnp.int32), TOP_K)
    n_pad = nb * blk - A
    pad_per = padded - counts
    padcum = jnp.cumsum(pad_per)
    q = jnp.arange(n_pad, dtype=jnp.int32)
    e_q = jnp.sum(q[:, None] >= padcum[None, :], axis=1).astype(jnp.int32)
    e_c = jnp.minimum(e_q, N_EXPERTS - 1)
    key_in = pstarts[e_c] + offset(counts[e_c] + q - (padcum - pad_per)[e_c], e_c)
    key_tail = pends[-1] - padcum[-1] + q
    pad_keys = jnp.where(e_q < N_EXPERTS, key_in, key_tail)
    keys = jnp.concatenate([dest.reshape(A), pad_keys])
    vals = jnp.concatenate([tok, pad_keys % n_tokens])
    _, tok_of_slot = lax.sort((keys, vals), num_keys=1)
    return dest, tok_of_slot, blk_expert


def kernel(x, c, positions, mod_w, mod_b, norm1_g, w_in, conv_a_w, w_a_out, ret_norm_g, w_b_out, glu_b, conv_c_w, conv_c_b, ln_c_g, ln_c_b, w_c_out, w_o, norm2_g, router_w, router_b, exp_w1, exp_b1, exp_w2, exp_b2, final_g):
    B, S, D = x.shape
    L = mod_w.shape[0]
    N = B * S
    da = w_a_out.shape[1]
    dr = w_b_out.shape[1]
    dc = w_c_out.shape[1]
    s1 = 3 * da
    s2 = s1 + 4 * dr
    s3 = s2 + 2 * dc

    mod = _modulation(c, mod_w, mod_b)
    cosf, sinf = _rope_tables(positions)
    tables = _retention_tables()
    row = lambda v: v.reshape(1, -1)
    rw_pad = jnp.zeros((L, D, LANES), F32).at[:, :, :N_EXPERTS].set(router_w)
    rb_pad = jnp.zeros((L, 1, LANES), F32).at[:, 0, :N_EXPERTS].set(router_b)

    for l in range(L):
        sh1, sc1, g1, sh2, sc2, g2 = [mod[l, :, i * D:(i + 1) * D].reshape(B, 1, D) for i in range(6)]
        wl = w_in[l].astype(BF16)
        wg = wl[:, s3:]
        n1 = row(norm1_g[l])
        ma = _branch_a(x, n1, sh1, sc1, wl[:, :s1], wg[:, :D], conv_a_w[l],
                       w_a_out[l].astype(BF16))
        mb = _branch_b(x, n1, sh1, sc1, cosf, sinf, wl[:, s1:s2], wg[:, D:2 * D], tables,
                       row(ret_norm_g[l]), w_b_out[l].astype(BF16))
        mc = _branch_c(x, n1, sh1, sc1, wl[:, s2:s3], wg[:, 2 * D:], row(glu_b[l]), conv_c_w[l],
                       row(conv_c_b[l]), row(ln_c_g[l]), row(ln_c_b[l]), w_c_out[l].astype(BF16))
        x1, idx, prob, rank, counts, *h2_parts = _out_router(
            ma, mb, mc, x, g1, w_o[l].astype(BF16), row(norm2_g[l]), sh2, sc2, rw_pad[l], rb_pad[l])
        idx = idx.reshape(N, LANES)[:, :TOP_K]
        rank = rank.reshape(N, LANES)[:, :TOP_K]
        dest, tok_of_slot, blk_expert = _routing_tables(counts[0, :N_EXPERTS], idx, rank, N)
        xparts = [_sc_row_gather(p, tok_of_slot) for p in h2_parts]
        yparts = _experts(xparts, blk_expert, exp_w1, exp_b1, exp_w2, exp_b2, layer=l)
        tmc = min(COMBINE_TILE, S)
        dest_kmajor = dest.reshape(N // tmc, tmc, TOP_K).transpose(0, 2, 1).reshape(N * TOP_K)
        ycat = [_sc_row_gather(p, dest_kmajor) for p in yparts]
        x = _combine(ycat, x1.reshape(N, D), g2, prob.reshape(N, LANES), row(final_g),
                     S, final_norm=(l == L - 1)).reshape(B, S, D)
    return x
```

```python
import functools

import jax
import jax.numpy as jnp
from jax import lax
from jax.experimental import pallas as pl
from jax.experimental.pallas import tpu as pltpu
from jax.experimental.pallas import tpu_sc as plsc

F32 = jnp.float32
BF16 = jnp.bfloat16

EPS = 1e-6
SHORT_CONV_W = 3
RET_HEADS = 8
RET_HEAD_DIM = 128
RET_CHUNK = 128
ROPE_BASE = 10000.0
CONF_KERNEL = 31
N_EXPERTS = 32
TOP_K = 4
SWIGLU_LIMIT = 7.0
SWIGLU_ALPHA = 1.702

LANES = 128
SUBLANES = 8
VMEM_LIMIT = 56 * 1024 * 1024

ROW_TILE = 512
EXPERT_BLOCK = 512
SC_GATHER_WINDOW = 128
MOE_SPLIT = 2
SC_SUBCORES = 32
SC_GATHER_ROWS = SC_GATHER_WINDOW * SC_SUBCORES
COMBINE_TILE = 512
CONV_ROWS = 32
CONF_HALO = 32


def _params(n_grid):
    return pltpu.CompilerParams(
        dimension_semantics=("arbitrary",) * n_grid, vmem_limit_bytes=VMEM_LIMIT)


def _const_spec(shape):
    nd = len(shape)
    return pl.BlockSpec(shape, lambda *_: (0,) * nd, pipeline_mode=pl.Buffered(1))


def _sigmoid(x):
    return 1.0 / (1.0 + jnp.exp(-x))


def _silu(x):
    return x * _sigmoid(x)


def _norm_mod(x, g, sh, sc):
    y = x * lax.rsqrt(jnp.mean(x * x, axis=-1, keepdims=True) + EPS)
    return (y * g) * (1.0 + sc) + sh


def _bdot(a, b):
    return jnp.dot(a.astype(BF16), b.astype(BF16), preferred_element_type=F32)


def _pack_rows(v):
    c = v.shape[1] // MOE_SPLIT
    parts = []
    for p in range(MOE_SPLIT):
        lo = v[:, p * c:p * c + c // 2].astype(BF16).astype(F32)
        hi = v[:, p * c + c // 2:(p + 1) * c].astype(BF16).astype(F32)
        lo_bits = lax.bitcast_convert_type(lo, jnp.uint32) >> 16
        hi_bits = lax.bitcast_convert_type(hi, jnp.uint32) & jnp.uint32(0xFFFF0000)
        parts.append(hi_bits | lo_bits)
    return parts


def _unpack_rows(parts):
    cols = []
    for w in parts:
        cols.append(lax.bitcast_convert_type(w << 16, F32))
        cols.append(lax.bitcast_convert_type(w & jnp.uint32(0xFFFF0000), F32))
    return jnp.concatenate(cols, axis=1)


def _mod_kernel(c_ref, w_ref, b_ref, o_ref):
    cond = _silu(c_ref[...])
    o_ref[0] = jnp.dot(cond, w_ref[0], preferred_element_type=F32,
                       precision=lax.Precision.HIGHEST) + b_ref[0]


def _modulation(c, mod_w, mod_b):
    L, D, M = mod_w.shape
    B = c.shape[0]
    tn = 1024
    return pl.pallas_call(
        _mod_kernel,
        grid=(L, M // tn),
        in_specs=[pl.BlockSpec((B, D), lambda l, j: (0, 0)),
                  pl.BlockSpec((1, D, tn), lambda l, j: (l, 0, j)),
                  pl.BlockSpec((1, 1, tn), lambda l, j: (l, 0, j))],
        out_specs=pl.BlockSpec((1, B, tn), lambda l, j: (l, 0, j)),
        out_shape=jax.ShapeDtypeStruct((L, B, M), F32),
        compiler_params=_params(2),
        name="modulation",
    )(c, mod_w, mod_b.reshape(L, 1, M))


def _rope_kernel(pos_ref, invf_ref, cos_ref, sin_ref):
    ang = pos_ref[0].astype(F32) * invf_ref[...]
    lane = lax.broadcasted_iota(jnp.int32, ang.shape, 1)
    s = jnp.sin(ang)
    cos_ref[0] = jnp.cos(ang)
    sin_ref[0] = jnp.where(lane < RET_HEAD_DIM // 2, -s, s)


def _rope_tables(positions):
    B, S = positions.shape
    inv_freq = 1.0 / (ROPE_BASE ** (jnp.arange(0, RET_HEAD_DIM, 2, dtype=F32) / RET_HEAD_DIM))
    invf = jnp.concatenate([inv_freq, inv_freq]).reshape(1, RET_HEAD_DIM)
    ts = min(S, 1024)
    spec = pl.BlockSpec((1, ts, RET_HEAD_DIM), lambda b, j: (b, j, 0))
    return pl.pallas_call(
        _rope_kernel,
        grid=(B, S // ts),
        in_specs=[pl.BlockSpec((1, ts, 1), lambda b, j: (b, j, 0)),
                  pl.BlockSpec((1, RET_HEAD_DIM), lambda b, j: (0, 0))],
        out_specs=[spec, spec],
        out_shape=[jax.ShapeDtypeStruct((B, S, RET_HEAD_DIM), F32)] * 2,
        compiler_params=_params(2),
        name="rope_tables",
    )(positions.reshape(B, S, 1), invf)


def _branch_a_kernel(x_ref, g_ref, sh_ref, sc_ref, w_ref, wg_ref, cw_ref, wo_ref, o_ref, buf_ref):
    tm = x_ref.shape[1]
    da = wo_ref.shape[0]

    @pl.when(pl.program_id(1) == 0)
    def _():
        buf_ref[0:SUBLANES, :] = jnp.zeros((SUBLANES, da), F32)

    h = _norm_mod(x_ref[0], g_ref[...], sh_ref[0], sc_ref[0]).astype(BF16)
    p = jnp.dot(h, w_ref[...], preferred_element_type=F32)
    b_g, c_g, xa = p[:, :da], p[:, da:2 * da], p[:, 2 * da:]
    u = c_g * xa
    buf_ref[SUBLANES:SUBLANES + tm, :] = u
    conv = cw_ref[2:3, :] * u
    for j in range(SHORT_CONV_W - 1):
        off = SUBLANES - (SHORT_CONV_W - 1) + j
        conv = conv + cw_ref[j:j + 1, :] * buf_ref[off:off + tm, :]
    buf_ref[0:SUBLANES, :] = buf_ref[tm:tm + SUBLANES, :]
    y_a = _bdot(b_g * conv, wo_ref[...])
    gate = _sigmoid(jnp.dot(h, wg_ref[...], preferred_element_type=F32))
    o_ref[0] = (gate * y_a).astype(BF16)


def _branch_a(x, g, sh, sc, w, wg, cw, wo):
    B, S, D = x.shape
    da = wo.shape[0]
    tm = min(ROW_TILE, S)
    vec = pl.BlockSpec((1, 1, D), lambda b, j: (b, 0, 0))
    return pl.pallas_call(
        _branch_a_kernel,
        grid=(B, S // tm),
        in_specs=[pl.BlockSpec((1, tm, D), lambda b, j: (b, j, 0)),
                  _const_spec((1, D)), vec, vec,
                  _const_spec(w.shape), _const_spec(wg.shape), _const_spec(cw.shape),
                  _const_spec(wo.shape)],
        out_specs=pl.BlockSpec((1, tm, D), lambda b, j: (b, j, 0)),
        out_shape=jax.ShapeDtypeStruct((B, S, D), BF16),
        scratch_shapes=[pltpu.VMEM((tm + SUBLANES, da), F32)],
        compiler_params=_params(2),
        name="branch_a",
    )(x, g, sh, sc, w, wg, cw, wo)


def _branch_c_kernel(x_ref, g_ref, sh_ref, sc_ref, w_ref, wg_ref, glub_ref, cw_ref, cb_ref,
                     lng_ref, lnb_ref, wo_ref, o_ref, buf_ref, conv_ref):
    tm = x_ref.shape[1]
    dc = wo_ref.shape[0]

    @pl.when(pl.program_id(1) == 0)
    def _():
        buf_ref[0, 0:CONF_HALO, :] = jnp.zeros((CONF_HALO, dc), F32)

    h = _norm_mod(x_ref[0], g_ref[...], sh_ref[0], sc_ref[0]).astype(BF16)
    p = jnp.dot(h, w_ref[...], preferred_element_type=F32) + glub_ref[...]
    buf_ref[0, CONF_HALO:CONF_HALO + tm, :] = p[:, :dc] * _sigmoid(p[:, dc:])
    for b in range(1, SUBLANES):
        buf_ref[b, SUBLANES:, :] = buf_ref[0, SUBLANES - b:tm + CONF_HALO - b, :]

    base = CONF_HALO - (CONF_KERNEL - 1)

    def strip(s, carry):
        r0 = pl.multiple_of(s * CONV_ROWS, CONV_ROWS)
        acc = jnp.zeros((CONV_ROWS, dc), F32)
        for j in range(CONF_KERNEL):
            a = -(-(base + j) // SUBLANES)
            b = a * SUBLANES - (base + j)
            acc = acc + cw_ref[j:j + 1, :] * buf_ref[b, pl.ds(r0 + a * SUBLANES, CONV_ROWS), :]
        conv_ref[pl.ds(r0, CONV_ROWS), :] = acc
        return carry

    lax.fori_loop(0, tm // CONV_ROWS, strip, 0)
    buf_ref[0, 0:CONF_HALO, :] = buf_ref[0, tm:tm + CONF_HALO, :]

    u = conv_ref[...] + cb_ref[...]
    mu = jnp.mean(u, axis=-1, keepdims=True)
    d = u - mu
    var = jnp.mean(d * d, axis=-1, keepdims=True)
    y = d * lax.rsqrt(var + EPS) * lng_ref[...] + lnb_ref[...]
    y_c = _bdot(_silu(y), wo_ref[...])
    gate = _sigmoid(jnp.dot(h, wg_ref[...], preferred_element_type=F32))
    o_ref[0] = (gate * y_c).astype(BF16)


def _branch_c(x, g, sh, sc, w, wg, glub, cw, cb, lng, lnb, wo):
    B, S, D = x.shape
    dc = wo.shape[0]
    tm = min(ROW_TILE, S)
    vec = pl.BlockSpec((1, 1, D), lambda b, j: (b, 0, 0))
    return pl.pallas_call(
        _branch_c_kernel,
        grid=(B, S // tm),
        in_specs=[pl.BlockSpec((1, tm, D), lambda b, j: (b, j, 0)),
                  _const_spec((1, D)), vec, vec,
                  _const_spec(w.shape), _const_spec(wg.shape), _const_spec(glub.shape),
                  _const_spec(cw.shape), _const_spec(cb.shape), _const_spec(lng.shape),
                  _const_spec(lnb.shape), _const_spec(wo.shape)],
        out_specs=pl.BlockSpec((1, tm, D), lambda b, j: (b, j, 0)),
        out_shape=jax.ShapeDtypeStruct((B, S, D), BF16),
        scratch_shapes=[pltpu.VMEM((SUBLANES, tm + CONF_HALO, dc), F32), pltpu.VMEM((tm, dc), F32)],
        compiler_params=_params(2),
        name="branch_c",
    )(x, g, sh, sc, w, wg, glub, cw, cb, lng, lnb, wo)


def _branch_b_kernel(x_ref, g_ref, sh_ref, sc_ref, cos_ref, sin_ref, w_ref, wg_ref,
                     dmask_ref, xi_ref, zeta_ref, gch_ref, rg_ref, wo_ref, o_ref,
                     q_ref, k_ref, v_ref, r_ref, state_ref):
    tm = x_ref.shape[1]
    dr = wo_ref.shape[0]
    hd = RET_HEAD_DIM

    @pl.when(pl.program_id(1) == 0)
    def _():
        state_ref[...] = jnp.zeros(state_ref.shape, F32)

    h = _norm_mod(x_ref[0], g_ref[...], sh_ref[0], sc_ref[0]).astype(BF16)
    cos = cos_ref[0]
    sin = sin_ref[0]
    scale = hd ** -0.5

    def rope(t):
        return t * cos + pltpu.roll(t, hd // 2, axis=1) * sin

    q = jnp.dot(h, w_ref[:, 0:dr], preferred_element_type=F32) * scale
    k = jnp.dot(h, w_ref[:, dr:2 * dr], preferred_element_type=F32)
    for hh in range(RET_HEADS):
        sl = slice(hh * hd, (hh + 1) * hd)
        q_ref[:, sl] = rope(q[:, sl]).astype(BF16)
        k_ref[:, sl] = rope(k[:, sl])
    v_ref[...] = jnp.dot(h, w_ref[:, 2 * dr:3 * dr], preferred_element_type=F32).astype(BF16)

    def chunk(ci, carry):
        r0 = pl.multiple_of(ci * RET_CHUNK, RET_CHUNK)
        rows = pl.ds(r0, RET_CHUNK)
        for hh in range(RET_HEADS):
            sl = slice(hh * hd, (hh + 1) * hd)
            qh = q_ref[rows, sl]
            kh = k_ref[rows, sl]
            vh = v_ref[rows, sl]
            st = state_ref[hh]
            scores = lax.dot_general(qh, kh.astype(BF16), (((1,), (1,)), ((), ())),
                                     preferred_element_type=F32) * dmask_ref[hh]
            inner = _bdot(scores, vh)
            cross = _bdot(qh, st) * xi_ref[hh]
            kz = (kh * zeta_ref[hh]).astype(BF16)
            state_ref[hh] = st * gch_ref[hh] + lax.dot_general(
                kz, vh, (((0,), (0,)), ((), ())), preferred_element_type=F32)
            o = inner + cross
            mu = jnp.mean(o, axis=-1, keepdims=True)
            d = o - mu
            var = jnp.mean(d * d, axis=-1, keepdims=True)
            r_ref[rows, sl] = d * lax.rsqrt(var + EPS)
        return carry

    lax.fori_loop(0, tm // RET_CHUNK, chunk, 0)

    og = jnp.dot(h, w_ref[:, 3 * dr:4 * dr], preferred_element_type=F32)
    y_b = _bdot(_silu(og) * (r_ref[...] * rg_ref[...]), wo_ref[...])
    gate = _sigmoid(jnp.dot(h, wg_ref[...], preferred_element_type=F32))
    o_ref[0] = (gate * y_b).astype(BF16)


def _retention_tables():
    H, C = RET_HEADS, RET_CHUNK
    gamma = 1.0 - jnp.power(2.0, -5.0 - jnp.arange(H, dtype=F32))
    lg = jnp.log(gamma)
    idx = jnp.arange(C, dtype=F32)
    rel = idx[:, None] - idx[None, :]
    dmask = jnp.where(rel[None] >= 0, jnp.exp(jnp.maximum(rel, 0.0)[None] * lg[:, None, None]), 0.0)
    xi = jnp.exp((idx + 1.0)[None] * lg[:, None])
    zeta = jnp.exp((C - 1.0 - idx)[None] * lg[:, None])
    g_chunk = jnp.exp(C * lg)
    bc = lambda t: jnp.broadcast_to(t[:, :, None], (H, C, RET_HEAD_DIM)).astype(F32)
    g_rows = jnp.broadcast_to(g_chunk[:, None, None], (H, 1, RET_HEAD_DIM)).astype(F32)
    return dmask.astype(F32), bc(xi), bc(zeta), g_rows


def _branch_b(x, g, sh, sc, cosf, sinf, w, wg, tables, rg, wo):
    B, S, D = x.shape
    dr = wo.shape[0]
    tm = min(ROW_TILE, S)
    dmask, xi, zeta, g_chunk = tables
    vec = pl.BlockSpec((1, 1, D), lambda b, j: (b, 0, 0))
    rope_spec = pl.BlockSpec((1, tm, RET_HEAD_DIM), lambda b, j: (b, j, 0))
    return pl.pallas_call(
        _branch_b_kernel,
        grid=(B, S // tm),
        in_specs=[pl.BlockSpec((1, tm, D), lambda b, j: (b, j, 0)),
                  _const_spec((1, D)), vec, vec, rope_spec, rope_spec,
                  _const_spec(w.shape), _const_spec(wg.shape), _const_spec(dmask.shape),
                  _const_spec(xi.shape), _const_spec(zeta.shape), _const_spec(g_chunk.shape),
                  _const_spec(rg.shape), _const_spec(wo.shape)],
        out_specs=pl.BlockSpec((1, tm, D), lambda b, j: (b, j, 0)),
        out_shape=jax.ShapeDtypeStruct((B, S, D), BF16),
        scratch_shapes=[pltpu.VMEM((tm, dr), BF16), pltpu.VMEM((tm, dr), F32),
                        pltpu.VMEM((tm, dr), BF16), pltpu.VMEM((tm, dr), F32),
                        pltpu.VMEM((RET_HEADS, RET_HEAD_DIM, RET_HEAD_DIM), F32)],
        compiler_params=_params(2),
        name="branch_b",
    )(x, g, sh, sc, cosf, sinf, w, wg, dmask, xi, zeta, g_chunk, rg, wo)


def _out_router_kernel(ma_ref, mb_ref, mc_ref, x_ref, g1_ref, wo_ref, n2_ref, sh_ref, sc_ref,
                       rw_ref, rb_ref, x1_ref, idx_ref, prob_ref, rank_ref, cnt_ref, *rest):
    h2_refs, carry_ref = rest[:MOE_SPLIT], rest[MOE_SPLIT]
    tm = x_ref.shape[1]
    first = jnp.logical_and(pl.program_id(0) == 0, pl.program_id(1) == 0)

    @pl.when(first)
    def _():
        carry_ref[...] = jnp.zeros(carry_ref.shape, F32)

    m = ma_ref[0].astype(F32) + mb_ref[0].astype(F32) + mc_ref[0].astype(F32)
    x1 = x_ref[0] + g1_ref[0] * _bdot(m, wo_ref[...])
    x1_ref[0] = x1
    h2 = _norm_mod(x1, n2_ref[...], sh_ref[0], sc_ref[0])
    for h2_ref, part in zip(h2_refs, _pack_rows(h2)):
        h2_ref[...] = part

    rw = rw_ref[...]
    h_hi = h2.astype(BF16)
    h_lo = (h2 - h_hi.astype(F32)).astype(BF16)
    rw_hi = rw.astype(BF16)
    rw_lo = (rw - rw_hi.astype(F32)).astype(BF16)
    logits = (jnp.dot(h_hi, rw_hi, preferred_element_type=F32)
              + jnp.dot(h_hi, rw_lo, preferred_element_type=F32)
              + jnp.dot(h_lo, rw_hi, preferred_element_type=F32)) + rb_ref[...]
    lane = lax.broadcasted_iota(jnp.int32, logits.shape, 1)
    lane_f = lane.astype(F32)
    neg = jnp.float32(-jnp.inf)
    work = jnp.where(lane < N_EXPERTS, logits, neg)
    idx_out = jnp.zeros(logits.shape, jnp.int32)
    val_out = jnp.full(logits.shape, neg, F32)
    onehot_all = jnp.zeros(logits.shape, F32)
    sels = []
    for kk in range(TOP_K):
        mx = jnp.max(work, axis=-1, keepdims=True)
        first_idx = jnp.min(jnp.where(work == mx, lane_f, float(LANES)), axis=-1, keepdims=True)
        sel = lane_f == first_idx
        sels.append(sel)
        idx_out = jnp.where(lane == kk, first_idx.astype(jnp.int32), idx_out)
        val_out = jnp.where(lane == kk, mx, val_out)
        onehot_all = onehot_all + sel.astype(F32)
        work = jnp.where(sel, neg, work)

    ex = jnp.exp(val_out - val_out[:, 0:1])
    prob_ref[0] = ex / jnp.sum(ex, axis=-1, keepdims=True)
    idx_ref[0] = idx_out

    row = lax.broadcasted_iota(jnp.int32, (tm, tm), 0)
    col = lax.broadcasted_iota(jnp.int32, (tm, tm), 1)
    lower = (col < row).astype(BF16)
    before = jnp.dot(lower, onehot_all.astype(BF16), preferred_element_type=F32) + carry_ref[...]
    rank_out = jnp.zeros(logits.shape, jnp.int32)
    for kk in range(TOP_K):
        rk = jnp.sum(jnp.where(sels[kk], before, 0.0), axis=-1, keepdims=True)
        rank_out = jnp.where(lane == kk, rk.astype(jnp.int32), rank_out)
    rank_ref[0] = rank_out
    total = carry_ref[...] + jnp.sum(onehot_all, axis=0, keepdims=True)
    carry_ref[...] = total
    cnt_ref[...] = total


def _out_router(ma, mb, mc, x, g1, wo, n2, sh2, sc2, rw, rb):
    B, S, D = x.shape
    tm = min(ROW_TILE, S)
    tile = pl.BlockSpec((1, tm, D), lambda b, j: (b, j, 0))
    vec = pl.BlockSpec((1, 1, D), lambda b, j: (b, 0, 0))
    lanes = pl.BlockSpec((1, tm, LANES), lambda b, j: (b, j, 0))
    dq = D // (2 * MOE_SPLIT)
    per_seq = S // tm
    part = pl.BlockSpec((tm, dq), lambda b, j: (b * per_seq + j, 0))
    return pl.pallas_call(
        _out_router_kernel,
        grid=(B, S // tm),
        in_specs=[tile, tile, tile, tile, vec, _const_spec(wo.shape), _const_spec((1, D)),
                  vec, vec, _const_spec(rw.shape), _const_spec(rb.shape)],
        out_specs=[tile, lanes, lanes, lanes, pl.BlockSpec((1, LANES), lambda b, j: (0, 0))]
        + [part] * MOE_SPLIT,
        out_shape=[jax.ShapeDtypeStruct((B, S, D), F32),
                   jax.ShapeDtypeStruct((B, S, LANES), jnp.int32),
                   jax.ShapeDtypeStruct((B, S, LANES), F32),
                   jax.ShapeDtypeStruct((B, S, LANES), jnp.int32),
                   jax.ShapeDtypeStruct((1, LANES), F32)]
        + [jax.ShapeDtypeStruct((B * S, dq), jnp.uint32)] * MOE_SPLIT,
        scratch_shapes=[pltpu.VMEM((1, LANES), F32)],
        compiler_params=_params(2),
        name="out_router",
    )(ma, mb, mc, x, g1, wo, n2, sh2, sc2, rw, rb)


def _sc_row_gather(src, indices):
    n = indices.shape[0]
    d = src.shape[1]
    w = SC_GATHER_WINDOW
    mesh = plsc.VectorSubcoreMesh(core_axis_name="core", subcore_axis_name="subcore")
    per = n // w // SC_SUBCORES
    window = lambda i: (i % per) * SC_SUBCORES + i // per

    @pl.kernel(out_type=jax.ShapeDtypeStruct((n, d), src.dtype), mesh=mesh, scratch_types=[])
    def gather_kernel(src_hbm, idx_hbm, out_hbm):
        def body(idx_vmem, out_vmem):
            pltpu.sync_copy(src_hbm.at[idx_vmem.at[0]], out_vmem)

        pltpu.emit_pipeline(
            body,
            grid=(n // w,),
            in_specs=[pl.BlockSpec((1, w), index_map=lambda i: (0, window(i)))],
            out_specs=[pl.BlockSpec((w, d), index_map=lambda i: (window(i), 0))],
            core_axis_name=("core", "subcore"),
            dimension_semantics=(pltpu.PARALLEL,),
        )(idx_hbm, out_hbm)

    return gather_kernel(src, indices.reshape(1, n))


def _expert_kernel(be_ref, *refs):
    x_refs, refs = refs[:MOE_SPLIT], refs[MOE_SPLIT:]
    w1_ref, b1_ref, w2_ref, b2_ref = refs[:4]
    o_refs, (w1b_ref, w2b_ref) = refs[4:4 + MOE_SPLIT], refs[4 + MOE_SPLIT:]
    i = pl.program_id(0)
    f = w2_ref.shape[2]

    @pl.when(jnp.logical_or(i == 0, be_ref[i] != be_ref[jnp.maximum(i - 1, 0)]))
    def _():
        w1b_ref[...] = w1_ref[0, 0].astype(BF16)
        w2b_ref[...] = w2_ref[0, 0].astype(BF16)

    xb = _unpack_rows([x_ref[...] for x_ref in x_refs]).astype(BF16)
    gu = jnp.dot(xb, w1b_ref[...], preferred_element_type=F32) + b1_ref[0, 0]
    glu = jnp.minimum(gu[:, :f], SWIGLU_LIMIT)
    lin = jnp.clip(gu[:, f:], -SWIGLU_LIMIT, SWIGLU_LIMIT)
    act = glu * _sigmoid(SWIGLU_ALPHA * glu) * (lin + 1.0)
    y = _bdot(act, w2b_ref[...]) + b2_ref[0, 0]
    for o_ref, part in zip(o_refs, _pack_rows(y)):
        o_ref[...] = part


def _experts(xparts, blk_expert, w1, b1, w2, b2, layer):
    rows, dq = xparts[0].shape
    L, E, D, F2 = w1.shape
    nb = blk_expert.shape[0]
    blk = EXPERT_BLOCK
    per_expert = lambda shape: pl.BlockSpec((1, 1) + shape, lambda i, be: (layer, be[i], 0, 0))
    grid_spec = pltpu.PrefetchScalarGridSpec(
        num_scalar_prefetch=1,
        grid=(nb,),
        in_specs=[pl.BlockSpec((blk, dq), lambda i, be: (i, 0))] * MOE_SPLIT
        + [per_expert((D, F2)), per_expert((1, F2)), per_expert((F2 // 2, D)), per_expert((1, D))],
        out_specs=[pl.BlockSpec((blk, dq), lambda i, be: (i, 0))] * MOE_SPLIT,
        scratch_shapes=[pltpu.VMEM((D, F2), BF16), pltpu.VMEM((F2 // 2, D), BF16)])
    return pl.pallas_call(
        _expert_kernel,
        grid_spec=grid_spec,
        out_shape=[jax.ShapeDtypeStruct((rows, dq), jnp.uint32)] * MOE_SPLIT,
        compiler_params=_params(1),
        name="experts",
    )(blk_expert, *xparts, w1, b1.reshape(L, E, 1, F2), w2, b2.reshape(L, E, 1, D))


def _combine_kernel(*refs, final_norm):
    y_refs, (x1_ref, g2_ref, prob_ref, fg_ref, o_ref) = refs[:MOE_SPLIT], refs[MOE_SPLIT:]
    tm = x1_ref.shape[0]
    prob = prob_ref[...]
    acc = jnp.zeros(x1_ref.shape, F32)
    for kk in range(TOP_K):
        rows = slice(kk * tm, (kk + 1) * tm)
        acc = acc + prob[:, kk:kk + 1] * _unpack_rows([y_ref[rows, :] for y_ref in y_refs])
    x2 = x1_ref[...] + g2_ref[0] * acc
    if final_norm:
        x2 = x2 * lax.rsqrt(jnp.mean(x2 * x2, axis=-1, keepdims=True) + EPS) * fg_ref[...]
    o_ref[...] = x2


def _combine(yparts, x1, g2, prob, final_g, seq_len, final_norm):
    N, D = x1.shape
    tm = min(COMBINE_TILE, seq_len)
    per_seq = seq_len // tm
    return pl.pallas_call(
        functools.partial(_combine_kernel, final_norm=final_norm),
        grid=(N // tm,),
        in_specs=[pl.BlockSpec((TOP_K * tm, D // (2 * MOE_SPLIT)), lambda i: (i, 0))] * MOE_SPLIT
        + [pl.BlockSpec((tm, D), lambda i: (i, 0)),
                  pl.BlockSpec((1, 1, D), lambda i: (i // per_seq, 0, 0)),
                  pl.BlockSpec((tm, LANES), lambda i: (i, 0)),
                  pl.BlockSpec((1, D), lambda i: (0, 0))],
        out_specs=pl.BlockSpec((tm, D), lambda i: (i, 0)),
        out_shape=jax.ShapeDtypeStruct((N, D), F32),
        compiler_params=_params(1),
        name="combine",
    )(*yparts, x1, g2, prob, final_g)


def _routing_tables(counts, idx, rank, n_tokens):
    blk = EXPERT_BLOCK
    A = n_tokens * TOP_K
    nb = (A + N_EXPERTS * (blk - 1)) // blk
    group = max(1, SC_GATHER_ROWS // blk)
    nb = -(-nb // group) * group
    counts = counts.astype(jnp.int32)
    padded = ((counts + blk - 1) // blk) * blk
    pends = jnp.cumsum(padded)
    pstarts = pends - padded
    block_start = jnp.arange(nb, dtype=jnp.int32) * blk
    blk_expert = jnp.minimum(jnp.sum(block_start[:, None] >= pends[None, :], axis=1),
                             N_EXPERTS - 1).astype(jnp.int32)
    nwin = padded // SC_GATHER_WINDOW
    def offset(r, e):
        return (r % SC_GATHER_WINDOW) * nwin[e] + r // SC_GATHER_WINDOW
    dest = pstarts[idx] + offset(rank, idx)
    tok = jnp.repeat(jnp.arange(n_tokens, dtype=jnp.int32), TOP_K)
    n_pad = nb * blk - A
    pad_per = padded - counts
    padcum = jnp.cumsum(pad_per)
    q = jnp.arange(n_pad, dtype=jnp.int32)
    e_q = jnp.sum(q[:, None] >= padcum[None, :], axis=1).astype(jnp.int32)
    e_c = jnp.minimum(e_q, N_EXPERTS - 1)
    key_in = pstarts[e_c] + offset(counts[e_c] + q - (padcum - pad_per)[e_c], e_c)
    key_tail = pends[-1] - padcum[-1] + q
    pad_keys = jnp.where(e_q < N_EXPERTS, key_in, key_tail)
    keys = jnp.concatenate([dest.reshape(A), pad_keys])
    vals = jnp.concatenate([tok, pad_keys % n_tokens])
    _, tok_of_slot = lax.sort((keys, vals), num_keys=1)
    return dest, tok_of_slot, blk_expert


def kernel(x, c, positions, mod_w, mod_b, norm1_g, w_in, conv_a_w, w_a_out, ret_norm_g, w_b_out, glu_b, conv_c_w, conv_c_b, ln_c_g, ln_c_b, w_c_out, w_o, norm2_g, router_w, router_b, exp_w1, exp_b1, exp_w2, exp_b2, final_g):
    B, S, D = x.shape
    L = mod_w.shape[0]
    N = B * S
    da = w_a_out.shape[1]
    dr = w_b_out.shape[1]
    dc = w_c_out.shape[1]
    s1 = 3 * da
    s2 = s1 + 4 * dr
    s3 = s2 + 2 * dc

    mod = _modulation(c, mod_w, mod_b)
    cosf, sinf = _rope_tables(positions)
    tables = _retention_tables()
    row = lambda v: v.reshape(1, -1)
    rw_pad = jnp.zeros((L, D, LANES), F32).at[:, :, :N_EXPERTS].set(router_w)
    rb_pad = jnp.zeros((L, 1, LANES), F32).at[:, 0, :N_EXPERTS].set(router_b)

    for l in range(L):
        sh1, sc1, g1, sh2, sc2, g2 = [mod[l, :, i * D:(i + 1) * D].reshape(B, 1, D) for i in range(6)]
        wl = w_in[l].astype(BF16)
        wg = wl[:, s3:]
        n1 = row(norm1_g[l])
        ma = _branch_a(x, n1, sh1, sc1, wl[:, :s1], wg[:, :D], conv_a_w[l],
                       w_a_out[l].astype(BF16))
        mb = _branch_b(x, n1, sh1, sc1, cosf, sinf, wl[:, s1:s2], wg[:, D:2 * D], tables,
                       row(ret_norm_g[l]), w_b_out[l].astype(BF16))
        mc = _branch_c(x, n1, sh1, sc1, wl[:, s2:s3], wg[:, 2 * D:], row(glu_b[l]), conv_c_w[l],
                       row(conv_c_b[l]), row(ln_c_g[l]), row(ln_c_b[l]), w_c_out[l].astype(BF16))
        x1, idx, prob, rank, counts, *h2_parts = _out_router(
            ma, mb, mc, x, g1, w_o[l].astype(BF16), row(norm2_g[l]), sh2, sc2, rw_pad[l], rb_pad[l])
        idx = idx.reshape(N, LANES)[:, :TOP_K]
        rank = rank.reshape(N, LANES)[:, :TOP_K]
        dest, tok_of_slot, blk_expert = _routing_tables(counts[0, :N_EXPERTS], idx, rank, N)
        xparts = [_sc_row_gather(p, tok_of_slot) for p in h2_parts]
        yparts = _experts(xparts, blk_expert, exp_w1, exp_b1, exp_w2, exp_b2, layer=l)
        tmc = min(COMBINE_TILE, S)
        dest_kmajor = dest.reshape(N // tmc, tmc, TOP_K).transpose(0, 2, 1).reshape(N * TOP_K)
        ycat = [_sc_row_gather(p, dest_kmajor) for p in yparts]
        x = _combine(ycat, x1.reshape(N, D), g2, prob.reshape(N, LANES), row(final_g),
                     S, final_norm=(l == L - 1)).reshape(B, S, D)
    return x
```

```python
import functools

import jax
import jax.numpy as jnp
from jax import lax
from jax.experimental import pallas as pl
from jax.experimental.pallas import tpu as pltpu
from jax.experimental.pallas import tpu_sc as plsc

F32 = jnp.float32
BF16 = jnp.bfloat16

EPS = 1e-6
SHORT_CONV_W = 3
RET_HEADS = 8
RET_HEAD_DIM = 128
RET_CHUNK = 128
ROPE_BASE = 10000.0
CONF_KERNEL = 31
N_EXPERTS = 32
TOP_K = 4
SWIGLU_LIMIT = 7.0
SWIGLU_ALPHA = 1.702

LANES = 128
SUBLANES = 8
VMEM_LIMIT = 56 * 1024 * 1024

ROW_TILE = 512
EXPERT_BLOCK = 512
SC_GATHER_WINDOW = 128
MOE_SPLIT = 2
SC_SUBCORES = 32
SC_GATHER_ROWS = SC_GATHER_WINDOW * SC_SUBCORES
COMBINE_TILE = 512
CONV_ROWS = 32
CONF_HALO = 32


def _params(n_grid):
    return pltpu.CompilerParams(
        dimension_semantics=("arbitrary",) * n_grid, vmem_limit_bytes=VMEM_LIMIT)


def _const_spec(shape):
    nd = len(shape)
    return pl.BlockSpec(shape, lambda *_: (0,) * nd, pipeline_mode=pl.Buffered(1))


def _sigmoid(x):
    return 1.0 / (1.0 + jnp.exp(-x))


def _silu(x):
    return x * _sigmoid(x)


def _norm_mod(x, g, sh, sc):
    y = x * lax.rsqrt(jnp.mean(x * x, axis=-1, keepdims=True) + EPS)
    return (y * g) * (1.0 + sc) + sh


def _bdot(a, b):
    return jnp.dot(a.astype(BF16), b.astype(BF16), preferred_element_type=F32)


def _pack_rows(v):
    c = v.shape[1] // MOE_SPLIT
    parts = []
    for p in range(MOE_SPLIT):
        lo = v[:, p * c:p * c + c // 2]
        hi = v[:, p * c + c // 2:(p + 1) * c]
        parts.append(pltpu.pack_elementwise([lo, hi], packed_dtype=BF16))
    return parts


def _unpack_rows(parts):
    cols = []
    for w in parts:
        for half in range(2):
            cols.append(pltpu.unpack_elementwise(w, index=half, packed_dtype=BF16,
                                                 unpacked_dtype=F32))
    return jnp.concatenate(cols, axis=1)


def _mod_kernel(c_ref, w_ref, b_ref, o_ref):
    cond = _silu(c_ref[...])
    o_ref[0] = jnp.dot(cond, w_ref[0], preferred_element_type=F32,
                       precision=lax.Precision.HIGHEST) + b_ref[0]


def _modulation(c, mod_w, mod_b):
    L, D, M = mod_w.shape
    B = c.shape[0]
    tn = 1024
    return pl.pallas_call(
        _mod_kernel,
        grid=(L, M // tn),
        in_specs=[pl.BlockSpec((B, D), lambda l, j: (0, 0)),
                  pl.BlockSpec((1, D, tn), lambda l, j: (l, 0, j)),
                  pl.BlockSpec((1, 1, tn), lambda l, j: (l, 0, j))],
        out_specs=pl.BlockSpec((1, B, tn), lambda l, j: (l, 0, j)),
        out_shape=jax.ShapeDtypeStruct((L, B, M), F32),
        compiler_params=_params(2),
        name="modulation",
    )(c, mod_w, mod_b.reshape(L, 1, M))


def _rope_kernel(pos_ref, invf_ref, cos_ref, sin_ref):
    ang = pos_ref[0].astype(F32) * invf_ref[...]
    lane = lax.broadcasted_iota(jnp.int32, ang.shape, 1)
    s = jnp.sin(ang)
    cos_ref[0] = jnp.cos(ang)
    sin_ref[0] = jnp.where(lane < RET_HEAD_DIM // 2, -s, s)


def _rope_tables(positions):
    B, S = positions.shape
    inv_freq = 1.0 / (ROPE_BASE ** (jnp.arange(0, RET_HEAD_DIM, 2, dtype=F32) / RET_HEAD_DIM))
    invf = jnp.concatenate([inv_freq, inv_freq]).reshape(1, RET_HEAD_DIM)
    ts = min(S, 1024)
    spec = pl.BlockSpec((1, ts, RET_HEAD_DIM), lambda b, j: (b, j, 0))
    return pl.pallas_call(
        _rope_kernel,
        grid=(B, S // ts),
        in_specs=[pl.BlockSpec((1, ts, 1), lambda b, j: (b, j, 0)),
                  pl.BlockSpec((1, RET_HEAD_DIM), lambda b, j: (0, 0))],
        out_specs=[spec, spec],
        out_shape=[jax.ShapeDtypeStruct((B, S, RET_HEAD_DIM), F32)] * 2,
        compiler_params=_params(2),
        name="rope_tables",
    )(positions.reshape(B, S, 1), invf)


def _branch_a_kernel(x_ref, g_ref, sh_ref, sc_ref, w_ref, wg_ref, cw_ref, wo_ref, o_ref, buf_ref):
    tm = x_ref.shape[1]
    da = wo_ref.shape[0]

    @pl.when(pl.program_id(1) == 0)
    def _():
        buf_ref[0:SUBLANES, :] = jnp.zeros((SUBLANES, da), F32)

    h = _norm_mod(x_ref[0], g_ref[...], sh_ref[0], sc_ref[0]).astype(BF16)
    p = jnp.dot(h, w_ref[...], preferred_element_type=F32)
    b_g, c_g, xa = p[:, :da], p[:, da:2 * da], p[:, 2 * da:]
    u = c_g * xa
    buf_ref[SUBLANES:SUBLANES + tm, :] = u
    conv = cw_ref[2:3, :] * u
    for j in range(SHORT_CONV_W - 1):
        off = SUBLANES - (SHORT_CONV_W - 1) + j
        conv = conv + cw_ref[j:j + 1, :] * buf_ref[off:off + tm, :]
    buf_ref[0:SUBLANES, :] = buf_ref[tm:tm + SUBLANES, :]
    y_a = _bdot(b_g * conv, wo_ref[...])
    gate = _sigmoid(jnp.dot(h, wg_ref[...], preferred_element_type=F32))
    o_ref[0] = (gate * y_a).astype(BF16)


def _branch_a(x, g, sh, sc, w, wg, cw, wo):
    B, S, D = x.shape
    da = wo.shape[0]
    tm = min(ROW_TILE, S)
    vec = pl.BlockSpec((1, 1, D), lambda b, j: (b, 0, 0))
    return pl.pallas_call(
        _branch_a_kernel,
        grid=(B, S // tm),
        in_specs=[pl.BlockSpec((1, tm, D), lambda b, j: (b, j, 0)),
                  _const_spec((1, D)), vec, vec,
                  _const_spec(w.shape), _const_spec(wg.shape), _const_spec(cw.shape),
                  _const_spec(wo.shape)],
        out_specs=pl.BlockSpec((1, tm, D), lambda b, j: (b, j, 0)),
        out_shape=jax.ShapeDtypeStruct((B, S, D), BF16),
        scratch_shapes=[pltpu.VMEM((tm + SUBLANES, da), F32)],
        compiler_params=_params(2),
        name="branch_a",
    )(x, g, sh, sc, w, wg, cw, wo)


def _branch_c_kernel(x_ref, g_ref, sh_ref, sc_ref, w_ref, wg_ref, glub_ref, cw_ref, cb_ref,
                     lng_ref, lnb_ref, wo_ref, o_ref, buf_ref, conv_ref):
    tm = x_ref.shape[1]
    dc = wo_ref.shape[0]

    @pl.when(pl.program_id(1) == 0)
    def _():
        buf_ref[0, 0:CONF_HALO, :] = jnp.zeros((CONF_HALO, dc), F32)

    h = _norm_mod(x_ref[0], g_ref[...], sh_ref[0], sc_ref[0]).astype(BF16)
    p = jnp.dot(h, w_ref[...], preferred_element_type=F32) + glub_ref[...]
    buf_ref[0, CONF_HALO:CONF_HALO + tm, :] = p[:, :dc] * _sigmoid(p[:, dc:])
    for b in range(1, SUBLANES):
        buf_ref[b, SUBLANES:, :] = buf_ref[0, SUBLANES - b:tm + CONF_HALO - b, :]

    base = CONF_HALO - (CONF_KERNEL - 1)

    def strip(s, carry):
        r0 = pl.multiple_of(s * CONV_ROWS, CONV_ROWS)
        acc = jnp.zeros((CONV_ROWS, dc), F32)
        for j in range(CONF_KERNEL):
            a = -(-(base + j) // SUBLANES)
            b = a * SUBLANES - (base + j)
            acc = acc + cw_ref[j:j + 1, :] * buf_ref[b, pl.ds(r0 + a * SUBLANES, CONV_ROWS), :]
        conv_ref[pl.ds(r0, CONV_ROWS), :] = acc
        return carry

    lax.fori_loop(0, tm // CONV_ROWS, strip, 0)
    buf_ref[0, 0:CONF_HALO, :] = buf_ref[0, tm:tm + CONF_HALO, :]

    u = conv_ref[...] + cb_ref[...]
    mu = jnp.mean(u, axis=-1, keepdims=True)
    d = u - mu
    var = jnp.mean(d * d, axis=-1, keepdims=True)
    y = d * lax.rsqrt(var + EPS) * lng_ref[...] + lnb_ref[...]
    y_c = _bdot(_silu(y), wo_ref[...])
    gate = _sigmoid(jnp.dot(h, wg_ref[...], preferred_element_type=F32))
    o_ref[0] = (gate * y_c).astype(BF16)


def _branch_c(x, g, sh, sc, w, wg, glub, cw, cb, lng, lnb, wo):
    B, S, D = x.shape
    dc = wo.shape[0]
    tm = min(ROW_TILE, S)
    vec = pl.BlockSpec((1, 1, D), lambda b, j: (b, 0, 0))
    return pl.pallas_call(
        _branch_c_kernel,
        grid=(B, S // tm),
        in_specs=[pl.BlockSpec((1, tm, D), lambda b, j: (b, j, 0)),
                  _const_spec((1, D)), vec, vec,
                  _const_spec(w.shape), _const_spec(wg.shape), _const_spec(glub.shape),
                  _const_spec(cw.shape), _const_spec(cb.shape), _const_spec(lng.shape),
                  _const_spec(lnb.shape), _const_spec(wo.shape)],
        out_specs=pl.BlockSpec((1, tm, D), lambda b, j: (b, j, 0)),
        out_shape=jax.ShapeDtypeStruct((B, S, D), BF16),
        scratch_shapes=[pltpu.VMEM((SUBLANES, tm + CONF_HALO, dc), F32), pltpu.VMEM((tm, dc), F32)],
        compiler_params=_params(2),
        name="branch_c",
    )(x, g, sh, sc, w, wg, glub, cw, cb, lng, lnb, wo)


def _branch_b_kernel(x_ref, g_ref, sh_ref, sc_ref, cos_ref, sin_ref, w_ref, wg_ref,
                     dmask_ref, xi_ref, zeta_ref, gch_ref, rg_ref, wo_ref, o_ref,
                     q_ref, k_ref, v_ref, r_ref, state_ref):
    tm = x_ref.shape[1]
    dr = wo_ref.shape[0]
    hd = RET_HEAD_DIM

    @pl.when(pl.program_id(1) == 0)
    def _():
        state_ref[...] = jnp.zeros(state_ref.shape, F32)

    h = _norm_mod(x_ref[0], g_ref[...], sh_ref[0], sc_ref[0]).astype(BF16)
    cos = cos_ref[0]
    sin = sin_ref[0]
    scale = hd ** -0.5

    def rope(t):
        return t * cos + pltpu.roll(t, hd // 2, axis=1) * sin

    q = jnp.dot(h, w_ref[:, 0:dr], preferred_element_type=F32) * scale
    k = jnp.dot(h, w_ref[:, dr:2 * dr], preferred_element_type=F32)
    for hh in range(RET_HEADS):
        sl = slice(hh * hd, (hh + 1) * hd)
        q_ref[:, sl] = rope(q[:, sl]).astype(BF16)
        k_ref[:, sl] = rope(k[:, sl])
    v_ref[...] = jnp.dot(h, w_ref[:, 2 * dr:3 * dr], preferred_element_type=F32).astype(BF16)

    def chunk(ci, carry):
        r0 = pl.multiple_of(ci * RET_CHUNK, RET_CHUNK)
        rows = pl.ds(r0, RET_CHUNK)
        for hh in range(RET_HEADS):
            sl = slice(hh * hd, (hh + 1) * hd)
            qh = q_ref[rows, sl]
            kh = k_ref[rows, sl]
            vh = v_ref[rows, sl]
            st = state_ref[hh]
            scores = lax.dot_general(qh, kh.astype(BF16), (((1,), (1,)), ((), ())),
                                     preferred_element_type=F32) * dmask_ref[hh]
            inner = _bdot(scores, vh)
            cross = _bdot(qh, st) * xi_ref[hh]
            kz = (kh * zeta_ref[hh]).astype(BF16)
            state_ref[hh] = st * gch_ref[hh] + lax.dot_general(
                kz, vh, (((0,), (0,)), ((), ())), preferred_element_type=F32)
            o = inner + cross
            mu = jnp.mean(o, axis=-1, keepdims=True)
            d = o - mu
            var = jnp.mean(d * d, axis=-1, keepdims=True)
            r_ref[rows, sl] = d * lax.rsqrt(var + EPS)
        return carry

    lax.fori_loop(0, tm // RET_CHUNK, chunk, 0)

    og = jnp.dot(h, w_ref[:, 3 * dr:4 * dr], preferred_element_type=F32)
    y_b = _bdot(_silu(og) * (r_ref[...] * rg_ref[...]), wo_ref[...])
    gate = _sigmoid(jnp.dot(h, wg_ref[...], preferred_element_type=F32))
    o_ref[0] = (gate * y_b).astype(BF16)


def _retention_tables():
    H, C = RET_HEADS, RET_CHUNK
    gamma = 1.0 - jnp.power(2.0, -5.0 - jnp.arange(H, dtype=F32))
    lg = jnp.log(gamma)
    idx = jnp.arange(C, dtype=F32)
    rel = idx[:, None] - idx[None, :]
    dmask = jnp.where(rel[None] >= 0, jnp.exp(jnp.maximum(rel, 0.0)[None] * lg[:, None, None]), 0.0)
    xi = jnp.exp((idx + 1.0)[None] * lg[:, None])
    zeta = jnp.exp((C - 1.0 - idx)[None] * lg[:, None])
    g_chunk = jnp.exp(C * lg)
    bc = lambda t: jnp.broadcast_to(t[:, :, None], (H, C, RET_HEAD_DIM)).astype(F32)
    g_rows = jnp.broadcast_to(g_chunk[:, None, None], (H, 1, RET_HEAD_DIM)).astype(F32)
    return dmask.astype(F32), bc(xi), bc(zeta), g_rows


def _branch_b(x, g, sh, sc, cosf, sinf, w, wg, tables, rg, wo):
    B, S, D = x.shape
    dr = wo.shape[0]
    tm = min(ROW_TILE, S)
    dmask, xi, zeta, g_chunk = tables
    vec = pl.BlockSpec((1, 1, D), lambda b, j: (b, 0, 0))
    rope_spec = pl.BlockSpec((1, tm, RET_HEAD_DIM), lambda b, j: (b, j, 0))
    return pl.pallas_call(
        _branch_b_kernel,
        grid=(B, S // tm),
        in_specs=[pl.BlockSpec((1, tm, D), lambda b, j: (b, j, 0)),
                  _const_spec((1, D)), vec, vec, rope_spec, rope_spec,
                  _const_spec(w.shape), _const_spec(wg.shape), _const_spec(dmask.shape),
                  _const_spec(xi.shape), _const_spec(zeta.shape), _const_spec(g_chunk.shape),
                  _const_spec(rg.shape), _const_spec(wo.shape)],
        out_specs=pl.BlockSpec((1, tm, D), lambda b, j: (b, j, 0)),
        out_shape=jax.ShapeDtypeStruct((B, S, D), BF16),
        scratch_shapes=[pltpu.VMEM((tm, dr), BF16), pltpu.VMEM((tm, dr), F32),
                        pltpu.VMEM((tm, dr), BF16), pltpu.VMEM((tm, dr), F32),
                        pltpu.VMEM((RET_HEADS, RET_HEAD_DIM, RET_HEAD_DIM), F32)],
        compiler_params=_params(2),
        name="branch_b",
    )(x, g, sh, sc, cosf, sinf, w, wg, dmask, xi, zeta, g_chunk, rg, wo)


def _out_router_kernel(ma_ref, mb_ref, mc_ref, x_ref, g1_ref, wo_ref, n2_ref, sh_ref, sc_ref,
                       rw_ref, rb_ref, x1_ref, idx_ref, prob_ref, rank_ref, cnt_ref, *rest):
    h2_refs, carry_ref = rest[:MOE_SPLIT], rest[MOE_SPLIT]
    tm = x_ref.shape[1]
    first = jnp.logical_and(pl.program_id(0) == 0, pl.program_id(1) == 0)

    @pl.when(first)
    def _():
        carry_ref[...] = jnp.zeros(carry_ref.shape, F32)

    m = ma_ref[0].astype(F32) + mb_ref[0].astype(F32) + mc_ref[0].astype(F32)
    x1 = x_ref[0] + g1_ref[0] * _bdot(m, wo_ref[...])
    x1_ref[0] = x1
    h2 = _norm_mod(x1, n2_ref[...], sh_ref[0], sc_ref[0])
    for h2_ref, part in zip(h2_refs, _pack_rows(h2)):
        h2_ref[...] = part

    rw = rw_ref[...]
    h_hi = h2.astype(BF16)
    h_lo = (h2 - h_hi.astype(F32)).astype(BF16)
    rw_hi = rw.astype(BF16)
    rw_lo = (rw - rw_hi.astype(F32)).astype(BF16)
    logits = (jnp.dot(h_hi, rw_hi, preferred_element_type=F32)
              + jnp.dot(h_hi, rw_lo, preferred_element_type=F32)
              + jnp.dot(h_lo, rw_hi, preferred_element_type=F32)) + rb_ref[...]
    lane = lax.broadcasted_iota(jnp.int32, logits.shape, 1)
    lane_f = lane.astype(F32)
    neg = jnp.float32(-jnp.inf)
    work = jnp.where(lane < N_EXPERTS, logits, neg)
    idx_out = jnp.zeros(logits.shape, jnp.int32)
    val_out = jnp.full(logits.shape, neg, F32)
    onehot_all = jnp.zeros(logits.shape, F32)
    sels = []
    for kk in range(TOP_K):
        mx = jnp.max(work, axis=-1, keepdims=True)
        first_idx = jnp.min(jnp.where(work == mx, lane_f, float(LANES)), axis=-1, keepdims=True)
        sel = lane_f == first_idx
        sels.append(sel)
        idx_out = jnp.where(lane == kk, first_idx.astype(jnp.int32), idx_out)
        val_out = jnp.where(lane == kk, mx, val_out)
        onehot_all = onehot_all + sel.astype(F32)
        work = jnp.where(sel, neg, work)

    ex = jnp.exp(val_out - val_out[:, 0:1])
    prob_ref[0] = ex / jnp.sum(ex, axis=-1, keepdims=True)
    idx_ref[0] = idx_out

    row = lax.broadcasted_iota(jnp.int32, (tm, tm), 0)
    col = lax.broadcasted_iota(jnp.int32, (tm, tm), 1)
    lower = (col < row).astype(BF16)
    before = jnp.dot(lower, onehot_all.astype(BF16), preferred_element_type=F32) + carry_ref[...]
    rank_out = jnp.zeros(logits.shape, jnp.int32)
    for kk in range(TOP_K):
        rk = jnp.sum(jnp.where(sels[kk], before, 0.0), axis=-1, keepdims=True)
        rank_out = jnp.where(lane == kk, rk.astype(jnp.int32), rank_out)
    rank_ref[0] = rank_out
    total = carry_ref[...] + jnp.sum(onehot_all, axis=0, keepdims=True)
    carry_ref[...] = total
    cnt_ref[...] = total


def _out_router(ma, mb, mc, x, g1, wo, n2, sh2, sc2, rw, rb):
    B, S, D = x.shape
    tm = min(ROW_TILE, S)
    tile = pl.BlockSpec((1, tm, D), lambda b, j: (b, j, 0))
    vec = pl.BlockSpec((1, 1, D), lambda b, j: (b, 0, 0))
    lanes = pl.BlockSpec((1, tm, LANES), lambda b, j: (b, j, 0))
    dq = D // (2 * MOE_SPLIT)
    per_seq = S // tm
    part = pl.BlockSpec((tm, dq), lambda b, j: (b * per_seq + j, 0))
    return pl.pallas_call(
        _out_router_kernel,
        grid=(B, S // tm),
        in_specs=[tile, tile, tile, tile, vec, _const_spec(wo.shape), _const_spec((1, D)),
                  vec, vec, _const_spec(rw.shape), _const_spec(rb.shape)],
        out_specs=[tile, lanes, lanes, lanes, pl.BlockSpec((1, LANES), lambda b, j: (0, 0))]
        + [part] * MOE_SPLIT,
        out_shape=[jax.ShapeDtypeStruct((B, S, D), F32),
                   jax.ShapeDtypeStruct((B, S, LANES), jnp.int32),
                   jax.ShapeDtypeStruct((B, S, LANES), F32),
                   jax.ShapeDtypeStruct((B, S, LANES), jnp.int32),
                   jax.ShapeDtypeStruct((1, LANES), F32)]
        + [jax.ShapeDtypeStruct((B * S, dq), jnp.uint32)] * MOE_SPLIT,
        scratch_shapes=[pltpu.VMEM((1, LANES), F32)],
        compiler_params=_params(2),
        name="out_router",
    )(ma, mb, mc, x, g1, wo, n2, sh2, sc2, rw, rb)


def _sc_row_gather(src, indices):
    n = indices.shape[0]
    d = src.shape[1]
    w = SC_GATHER_WINDOW
    mesh = plsc.VectorSubcoreMesh(core_axis_name="core", subcore_axis_name="subcore")
    per = n // w // SC_SUBCORES
    window = lambda i: (i % per) * SC_SUBCORES + i // per

    @pl.kernel(out_type=jax.ShapeDtypeStruct((n, d), src.dtype), mesh=mesh, scratch_types=[])
    def gather_kernel(src_hbm, idx_hbm, out_hbm):
        def body(idx_vmem, out_vmem):
            pltpu.sync_copy(src_hbm.at[idx_vmem.at[0]], out_vmem)

        pltpu.emit_pipeline(
            body,
            grid=(n // w,),
            in_specs=[pl.BlockSpec((1, w), index_map=lambda i: (0, window(i)))],
            out_specs=[pl.BlockSpec((w, d), index_map=lambda i: (window(i), 0))],
            core_axis_name=("core", "subcore"),
            dimension_semantics=(pltpu.PARALLEL,),
        )(idx_hbm, out_hbm)

    return gather_kernel(src, indices.reshape(1, n))


def _expert_kernel(be_ref, *refs):
    x_refs, refs = refs[:MOE_SPLIT], refs[MOE_SPLIT:]
    w1_ref, b1_ref, w2_ref, b2_ref = refs[:4]
    o_refs, (w1b_ref, w2b_ref) = refs[4:4 + MOE_SPLIT], refs[4 + MOE_SPLIT:]
    i = pl.program_id(0)
    f = w2_ref.shape[2]

    @pl.when(jnp.logical_or(i == 0, be_ref[i] != be_ref[jnp.maximum(i - 1, 0)]))
    def _():
        w1b_ref[...] = w1_ref[0, 0].astype(BF16)
        w2b_ref[...] = w2_ref[0, 0].astype(BF16)

    xb = _unpack_rows([x_ref[...] for x_ref in x_refs]).astype(BF16)
    gu = jnp.dot(xb, w1b_ref[...], preferred_element_type=F32) + b1_ref[0, 0]
    glu = jnp.minimum(gu[:, :f], SWIGLU_LIMIT)
    lin = jnp.clip(gu[:, f:], -SWIGLU_LIMIT, SWIGLU_LIMIT)
    act = glu * _sigmoid(SWIGLU_ALPHA * glu) * (lin + 1.0)
    y = _bdot(act, w2b_ref[...]) + b2_ref[0, 0]
    for o_ref, part in zip(o_refs, _pack_rows(y)):
        o_ref[...] = part


def _experts(xparts, blk_expert, w1, b1, w2, b2, layer):
    rows, dq = xparts[0].shape
    L, E, D, F2 = w1.shape
    nb = blk_expert.shape[0]
    blk = EXPERT_BLOCK
    per_expert = lambda shape: pl.BlockSpec((1, 1) + shape, lambda i, be: (layer, be[i], 0, 0))
    grid_spec = pltpu.PrefetchScalarGridSpec(
        num_scalar_prefetch=1,
        grid=(nb,),
        in_specs=[pl.BlockSpec((blk, dq), lambda i, be: (i, 0))] * MOE_SPLIT
        + [per_expert((D, F2)), per_expert((1, F2)), per_expert((F2 // 2, D)), per_expert((1, D))],
        out_specs=[pl.BlockSpec((blk, dq), lambda i, be: (i, 0))] * MOE_SPLIT,
        scratch_shapes=[pltpu.VMEM((D, F2), BF16), pltpu.VMEM((F2 // 2, D), BF16)])
    return pl.pallas_call(
        _expert_kernel,
        grid_spec=grid_spec,
        out_shape=[jax.ShapeDtypeStruct((rows, dq), jnp.uint32)] * MOE_SPLIT,
        compiler_params=_params(1),
        name="experts",
    )(blk_expert, *xparts, w1, b1.reshape(L, E, 1, F2), w2, b2.reshape(L, E, 1, D))


def _combine_kernel(*refs, final_norm):
    y_refs, (x1_ref, g2_ref, prob_ref, fg_ref, o_ref) = refs[:MOE_SPLIT], refs[MOE_SPLIT:]
    tm = x1_ref.shape[0]
    prob = prob_ref[...]
    acc = jnp.zeros(x1_ref.shape, F32)
    for kk in range(TOP_K):
        rows = slice(kk * tm, (kk + 1) * tm)
        acc = acc + prob[:, kk:kk + 1] * _unpack_rows([y_ref[rows, :] for y_ref in y_refs])
    x2 = x1_ref[...] + g2_ref[0] * acc
    if final_norm:
        x2 = x2 * lax.rsqrt(jnp.mean(x2 * x2, axis=-1, keepdims=True) + EPS) * fg_ref[...]
    o_ref[...] = x2


def _combine(yparts, x1, g2, prob, final_g, seq_len, final_norm):
    N, D = x1.shape
    tm = min(COMBINE_TILE, seq_len)
    per_seq = seq_len // tm
    return pl.pallas_call(
        functools.partial(_combine_kernel, final_norm=final_norm),
        grid=(N // tm,),
        in_specs=[pl.BlockSpec((TOP_K * tm, D // (2 * MOE_SPLIT)), lambda i: (i, 0))] * MOE_SPLIT
        + [pl.BlockSpec((tm, D), lambda i: (i, 0)),
                  pl.BlockSpec((1, 1, D), lambda i: (i // per_seq, 0, 0)),
                  pl.BlockSpec((tm, LANES), lambda i: (i, 0)),
                  pl.BlockSpec((1, D), lambda i: (0, 0))],
        out_specs=pl.BlockSpec((tm, D), lambda i: (i, 0)),
        out_shape=jax.ShapeDtypeStruct((N, D), F32),
        compiler_params=_params(1),
        name="combine",
    )(*yparts, x1, g2, prob, final_g)


def _routing_tables(counts, idx, rank, n_tokens):
    blk = EXPERT_BLOCK
    A = n_tokens * TOP_K
    nb = (A + N_EXPERTS * (blk - 1)) // blk
    group = max(1, SC_GATHER_ROWS // blk)
    nb = -(-nb // group) * group
    counts = counts.astype(jnp.int32)
    padded = ((counts + blk - 1) // blk) * blk
    pends = jnp.cumsum(padded)
    pstarts = pends - padded
    block_start = jnp.arange(nb, dtype=jnp.int32) * blk
    blk_expert = jnp.minimum(jnp.sum(block_start[:, None] >= pends[None, :], axis=1),
                             N_EXPERTS - 1).astype(jnp.int32)
    nwin = padded // SC_GATHER_WINDOW
    def offset(r, e):
        return (r % SC_GATHER_WINDOW) * nwin[e] + r // SC_GATHER_WINDOW
    dest = pstarts[idx] + offset(rank, idx)
    tok = jnp.repeat(jnp.arange(n_tokens, dtype=jnp.int32), TOP_K)
    n_pad = nb * blk - A
    pad_per = padded - counts
    padcum = jnp.cumsum(pad_per)
    q = jnp.arange(n_pad, dtype=jnp.int32)
    e_q = jnp.sum(q[:, None] >= padcum[None, :], axis=1).astype(jnp.int32)
    e_c = jnp.minimum(e_q, N_EXPERTS - 1)
    key_in = pstarts[e_c] + offset(counts[e_c] + q - (padcum - pad_per)[e_c], e_c)
    key_tail = pends[-1] - padcum[-1] + q
    pad_keys = jnp.where(e_q < N_EXPERTS, key_in, key_tail)
    keys = jnp.concatenate([dest.reshape(A), pad_keys])
    vals = jnp.concatenate([tok, pad_keys % n_tokens])
    _, tok_of_slot = lax.sort((keys, vals), num_keys=1)
    return dest, tok_of_slot, blk_expert


def kernel(x, c, positions, mod_w, mod_b, norm1_g, w_in, conv_a_w, w_a_out, ret_norm_g, w_b_out, glu_b, conv_c_w, conv_c_b, ln_c_g, ln_c_b, w_c_out, w_o, norm2_g, router_w, router_b, exp_w1, exp_b1, exp_w2, exp_b2, final_g):
    B, S, D = x.shape
    L = mod_w.shape[0]
    N = B * S
    da = w_a_out.shape[1]
    dr = w_b_out.shape[1]
    dc = w_c_out.shape[1]
    s1 = 3 * da
    s2 = s1 + 4 * dr
    s3 = s2 + 2 * dc

    mod = _modulation(c, mod_w, mod_b)
    cosf, sinf = _rope_tables(positions)
    tables = _retention_tables()
    row = lambda v: v.reshape(1, -1)
    rw_pad = jnp.zeros((L, D, LANES), F32).at[:, :, :N_EXPERTS].set(router_w)
    rb_pad = jnp.zeros((L, 1, LANES), F32).at[:, 0, :N_EXPERTS].set(router_b)

    for l in range(L):
        sh1, sc1, g1, sh2, sc2, g2 = [mod[l, :, i * D:(i + 1) * D].reshape(B, 1, D) for i in range(6)]
        wl = w_in[l].astype(BF16)
        wg = wl[:, s3:]
        n1 = row(norm1_g[l])
        ma = _branch_a(x, n1, sh1, sc1, wl[:, :s1], wg[:, :D], conv_a_w[l],
                       w_a_out[l].astype(BF16))
        mb = _branch_b(x, n1, sh1, sc1, cosf, sinf, wl[:, s1:s2], wg[:, D:2 * D], tables,
                       row(ret_norm_g[l]), w_b_out[l].astype(BF16))
        mc = _branch_c(x, n1, sh1, sc1, wl[:, s2:s3], wg[:, 2 * D:], row(glu_b[l]), conv_c_w[l],
                       row(conv_c_b[l]), row(ln_c_g[l]), row(ln_c_b[l]), w_c_out[l].astype(BF16))
        x1, idx, prob, rank, counts, *h2_parts = _out_router(
            ma, mb, mc, x, g1, w_o[l].astype(BF16), row(norm2_g[l]), sh2, sc2, rw_pad[l], rb_pad[l])
        idx = idx.reshape(N, LANES)[:, :TOP_K]
        rank = rank.reshape(N, LANES)[:, :TOP_K]
        dest, tok_of_slot, blk_expert = _routing_tables(counts[0, :N_EXPERTS], idx, rank, N)
        xparts = [_sc_row_gather(p, tok_of_slot) for p in h2_parts]
        yparts = _experts(xparts, blk_expert, exp_w1, exp_b1, exp_w2, exp_b2, layer=l)
        tmc = min(COMBINE_TILE, S)
        dest_kmajor = dest.reshape(N // tmc, tmc, TOP_K).transpose(0, 2, 1).reshape(N * TOP_K)
        ycat = [_sc_row_gather(p, dest_kmajor) for p in yparts]
        x = _combine(ycat, x1.reshape(N, D), g2, prob.reshape(N, LANES), row(final_g),
                     S, final_norm=(l == L - 1)).reshape(B, S, D)
    return x
```

```python
import functools

import jax
import jax.numpy as jnp
from jax import lax
from jax.experimental import pallas as pl
from jax.experimental.pallas import tpu as pltpu
from jax.experimental.pallas import tpu_sc as plsc

F32 = jnp.float32
BF16 = jnp.bfloat16

EPS = 1e-6
SHORT_CONV_W = 3
RET_HEADS = 8
RET_HEAD_DIM = 128
RET_CHUNK = 128
ROPE_BASE = 10000.0
CONF_KERNEL = 31
N_EXPERTS = 32
TOP_K = 4
SWIGLU_LIMIT = 7.0
SWIGLU_ALPHA = 1.702

LANES = 128
SUBLANES = 8
VMEM_LIMIT = 56 * 1024 * 1024

ROW_TILE = 512
EXPERT_BLOCK = 512
SC_GATHER_WINDOW = 128
MOE_SPLIT = 2
SC_SUBCORES = 32
SC_GATHER_ROWS = SC_GATHER_WINDOW * SC_SUBCORES
COMBINE_TILE = 512
CONV_ROWS = 32
CONF_HALO = 32


def _params(n_grid):
    return pltpu.CompilerParams(
        dimension_semantics=("arbitrary",) * n_grid, vmem_limit_bytes=VMEM_LIMIT)


def _const_spec(shape):
    nd = len(shape)
    return pl.BlockSpec(shape, lambda *_: (0,) * nd, pipeline_mode=pl.Buffered(1))


def _sigmoid(x):
    return 1.0 / (1.0 + jnp.exp(-x))


def _silu(x):
    return x * _sigmoid(x)


def _norm_mod(x, g, sh, sc):
    y = x * lax.rsqrt(jnp.mean(x * x, axis=-1, keepdims=True) + EPS)
    return (y * g) * (1.0 + sc) + sh


def _bdot(a, b):
    return jnp.dot(a.astype(BF16), b.astype(BF16), preferred_element_type=F32)


def _pack_rows(v):
    c = v.shape[1] // MOE_SPLIT
    parts = []
    for p in range(MOE_SPLIT):
        lo = v[:, p * c:p * c + c // 2].astype(BF16).astype(F32)
        hi = v[:, p * c + c // 2:(p + 1) * c].astype(BF16).astype(F32)
        lo_bits = lax.bitcast_convert_type(lo, jnp.uint32) >> 16
        hi_bits = lax.bitcast_convert_type(hi, jnp.uint32) & jnp.uint32(0xFFFF0000)
        parts.append(hi_bits | lo_bits)
    return parts


def _unpack_rows(parts):
    cols = []
    for w in parts:
        cols.append(lax.bitcast_convert_type(w << 16, F32))
        cols.append(lax.bitcast_convert_type(w & jnp.uint32(0xFFFF0000), F32))
    return jnp.concatenate(cols, axis=1)


def _mod_kernel(c_ref, w_ref, b_ref, o_ref):
    cond = _silu(c_ref[...])
    o_ref[0] = jnp.dot(cond, w_ref[0], preferred_element_type=F32,
                       precision=lax.Precision.HIGHEST) + b_ref[0]


def _modulation(c, mod_w, mod_b):
    L, D, M = mod_w.shape
    B = c.shape[0]
    tn = 1024
    return pl.pallas_call(
        _mod_kernel,
        grid=(L, M // tn),
        in_specs=[pl.BlockSpec((B, D), lambda l, j: (0, 0)),
                  pl.BlockSpec((1, D, tn), lambda l, j: (l, 0, j)),
                  pl.BlockSpec((1, 1, tn), lambda l, j: (l, 0, j))],
        out_specs=pl.BlockSpec((1, B, tn), lambda l, j: (l, 0, j)),
        out_shape=jax.ShapeDtypeStruct((L, B, M), F32),
        compiler_params=_params(2),
        name="modulation",
    )(c, mod_w, mod_b.reshape(L, 1, M))


def _rope_kernel(pos_ref, invf_ref, cos_ref, sin_ref):
    ang = pos_ref[0].astype(F32) * invf_ref[...]
    lane = lax.broadcasted_iota(jnp.int32, ang.shape, 1)
    s = jnp.sin(ang)
    cos_ref[0] = jnp.cos(ang)
    sin_ref[0] = jnp.where(lane < RET_HEAD_DIM // 2, -s, s)


def _rope_tables(positions):
    B, S = positions.shape
    inv_freq = 1.0 / (ROPE_BASE ** (jnp.arange(0, RET_HEAD_DIM, 2, dtype=F32) / RET_HEAD_DIM))
    invf = jnp.concatenate([inv_freq, inv_freq]).reshape(1, RET_HEAD_DIM)
    ts = min(S, 1024)
    spec = pl.BlockSpec((1, ts, RET_HEAD_DIM), lambda b, j: (b, j, 0))
    return pl.pallas_call(
        _rope_kernel,
        grid=(B, S // ts),
        in_specs=[pl.BlockSpec((1, ts, 1), lambda b, j: (b, j, 0)),
                  pl.BlockSpec((1, RET_HEAD_DIM), lambda b, j: (0, 0))],
        out_specs=[spec, spec],
        out_shape=[jax.ShapeDtypeStruct((B, S, RET_HEAD_DIM), F32)] * 2,
        compiler_params=_params(2),
        name="rope_tables",
    )(positions.reshape(B, S, 1), invf)


def _branch_a_kernel(x_ref, g_ref, sh_ref, sc_ref, w_ref, wg_ref, cw_ref, wo_ref, o_ref, buf_ref):
    tm = x_ref.shape[1]
    da = wo_ref.shape[0]

    @pl.when(pl.program_id(1) == 0)
    def _():
        buf_ref[0:SUBLANES, :] = jnp.zeros((SUBLANES, da), F32)

    h = _norm_mod(x_ref[0], g_ref[...], sh_ref[0], sc_ref[0]).astype(BF16)
    p = jnp.dot(h, w_ref[...], preferred_element_type=F32)
    b_g, c_g, xa = p[:, :da], p[:, da:2 * da], p[:, 2 * da:]
    u = c_g * xa
    buf_ref[SUBLANES:SUBLANES + tm, :] = u
    conv = cw_ref[2:3, :] * u
    for j in range(SHORT_CONV_W - 1):
        off = SUBLANES - (SHORT_CONV_W - 1) + j
        conv = conv + cw_ref[j:j + 1, :] * buf_ref[off:off + tm, :]
    buf_ref[0:SUBLANES, :] = buf_ref[tm:tm + SUBLANES, :]
    y_a = _bdot(b_g * conv, wo_ref[...])
    gate = _sigmoid(jnp.dot(h, wg_ref[...], preferred_element_type=F32))
    o_ref[0] = (gate * y_a).astype(BF16)


def _branch_a(x, g, sh, sc, w, wg, cw, wo):
    B, S, D = x.shape
    da = wo.shape[0]
    tm = min(ROW_TILE, S)
    vec = pl.BlockSpec((1, 1, D), lambda b, j: (b, 0, 0))
    return pl.pallas_call(
        _branch_a_kernel,
        grid=(B, S // tm),
        in_specs=[pl.BlockSpec((1, tm, D), lambda b, j: (b, j, 0)),
                  _const_spec((1, D)), vec, vec,
                  _const_spec(w.shape), _const_spec(wg.shape), _const_spec(cw.shape),
                  _const_spec(wo.shape)],
        out_specs=pl.BlockSpec((1, tm, D), lambda b, j: (b, j, 0)),
        out_shape=jax.ShapeDtypeStruct((B, S, D), BF16),
        scratch_shapes=[pltpu.VMEM((tm + SUBLANES, da), F32)],
        compiler_params=_params(2),
        name="branch_a",
    )(x, g, sh, sc, w, wg, cw, wo)


def _branch_c_kernel(x_ref, g_ref, sh_ref, sc_ref, w_ref, wg_ref, glub_ref, cw_ref, cb_ref,
                     lng_ref, lnb_ref, wo_ref, o_ref, buf_ref, conv_ref):
    tm = x_ref.shape[1]
    dc = wo_ref.shape[0]

    @pl.when(pl.program_id(1) == 0)
    def _():
        buf_ref[0, 0:CONF_HALO, :] = jnp.zeros((CONF_HALO, dc), F32)

    h = _norm_mod(x_ref[0], g_ref[...], sh_ref[0], sc_ref[0]).astype(BF16)
    p = jnp.dot(h, w_ref[...], preferred_element_type=F32) + glub_ref[...]
    buf_ref[0, CONF_HALO:CONF_HALO + tm, :] = p[:, :dc] * _sigmoid(p[:, dc:])
    for b in range(1, SUBLANES):
        buf_ref[b, SUBLANES:, :] = buf_ref[0, SUBLANES - b:tm + CONF_HALO - b, :]

    base = CONF_HALO - (CONF_KERNEL - 1)

    def strip(s, carry):
        r0 = pl.multiple_of(s * CONV_ROWS, CONV_ROWS)
        acc = jnp.zeros((CONV_ROWS, dc), F32)
        for j in range(CONF_KERNEL):
            a = -(-(base + j) // SUBLANES)
            b = a * SUBLANES - (base + j)
            acc = acc + cw_ref[j:j + 1, :] * buf_ref[b, pl.ds(r0 + a * SUBLANES, CONV_ROWS), :]
        conv_ref[pl.ds(r0, CONV_ROWS), :] = acc
        return carry

    lax.fori_loop(0, tm // CONV_ROWS, strip, 0)
    buf_ref[0, 0:CONF_HALO, :] = buf_ref[0, tm:tm + CONF_HALO, :]

    u = conv_ref[...] + cb_ref[...]
    mu = jnp.mean(u, axis=-1, keepdims=True)
    d = u - mu
    var = jnp.mean(d * d, axis=-1, keepdims=True)
    y = d * lax.rsqrt(var + EPS) * lng_ref[...] + lnb_ref[...]
    y_c = _bdot(_silu(y), wo_ref[...])
    gate = _sigmoid(jnp.dot(h, wg_ref[...], preferred_element_type=F32))
    o_ref[0] = (gate * y_c).astype(BF16)


def _branch_c(x, g, sh, sc, w, wg, glub, cw, cb, lng, lnb, wo):
    B, S, D = x.shape
    dc = wo.shape[0]
    tm = min(ROW_TILE, S)
    vec = pl.BlockSpec((1, 1, D), lambda b, j: (b, 0, 0))
    return pl.pallas_call(
        _branch_c_kernel,
        grid=(B, S // tm),
        in_specs=[pl.BlockSpec((1, tm, D), lambda b, j: (b, j, 0)),
                  _const_spec((1, D)), vec, vec,
                  _const_spec(w.shape), _const_spec(wg.shape), _const_spec(glub.shape),
                  _const_spec(cw.shape), _const_spec(cb.shape), _const_spec(lng.shape),
                  _const_spec(lnb.shape), _const_spec(wo.shape)],
        out_specs=pl.BlockSpec((1, tm, D), lambda b, j: (b, j, 0)),
        out_shape=jax.ShapeDtypeStruct((B, S, D), BF16),
        scratch_shapes=[pltpu.VMEM((SUBLANES, tm + CONF_HALO, dc), F32), pltpu.VMEM((tm, dc), F32)],
        compiler_params=_params(2),
        name="branch_c",
    )(x, g, sh, sc, w, wg, glub, cw, cb, lng, lnb, wo)


def _branch_b_kernel(x_ref, g_ref, sh_ref, sc_ref, cos_ref, sin_ref, w_ref, wg_ref,
                     dmask_ref, xi_ref, zeta_ref, gch_ref, rg_ref, wo_ref, o_ref,
                     q_ref, k_ref, v_ref, r_ref, state_ref):
    tm = x_ref.shape[1]
    dr = wo_ref.shape[0]
    hd = RET_HEAD_DIM

    @pl.when(pl.program_id(1) == 0)
    def _():
        state_ref[...] = jnp.zeros(state_ref.shape, F32)

    h = _norm_mod(x_ref[0], g_ref[...], sh_ref[0], sc_ref[0]).astype(BF16)
    cos = cos_ref[0]
    sin = sin_ref[0]
    scale = hd ** -0.5

    def rope(t):
        return t * cos + pltpu.roll(t, hd // 2, axis=1) * sin

    q = jnp.dot(h, w_ref[:, 0:dr], preferred_element_type=F32) * scale
    k = jnp.dot(h, w_ref[:, dr:2 * dr], preferred_element_type=F32)
    for hh in range(RET_HEADS):
        sl = slice(hh * hd, (hh + 1) * hd)
        q_ref[:, sl] = rope(q[:, sl]).astype(BF16)
        k_ref[:, sl] = rope(k[:, sl])
    v_ref[...] = jnp.dot(h, w_ref[:, 2 * dr:3 * dr], preferred_element_type=F32).astype(BF16)

    for ci in range(tm // RET_CHUNK):
        rows = pl.ds(ci * RET_CHUNK, RET_CHUNK)
        for hh in range(RET_HEADS):
            sl = slice(hh * hd, (hh + 1) * hd)
            qh = q_ref[rows, sl]
            kh = k_ref[rows, sl]
            vh = v_ref[rows, sl]
            st = state_ref[hh]
            scores = lax.dot_general(qh, kh.astype(BF16), (((1,), (1,)), ((), ())),
                                     preferred_element_type=F32) * dmask_ref[hh]
            inner = _bdot(scores, vh)
            cross = _bdot(qh, st) * xi_ref[hh]
            kz = (kh * zeta_ref[hh]).astype(BF16)
            state_ref[hh] = st * gch_ref[hh] + lax.dot_general(
                kz, vh, (((0,), (0,)), ((), ())), preferred_element_type=F32)
            o = inner + cross
            mu = jnp.mean(o, axis=-1, keepdims=True)
            d = o - mu
            var = jnp.mean(d * d, axis=-1, keepdims=True)
            r_ref[rows, sl] = d * lax.rsqrt(var + EPS)

    og = jnp.dot(h, w_ref[:, 3 * dr:4 * dr], preferred_element_type=F32)
    y_b = _bdot(_silu(og) * (r_ref[...] * rg_ref[...]), wo_ref[...])
    gate = _sigmoid(jnp.dot(h, wg_ref[...], preferred_element_type=F32))
    o_ref[0] = (gate * y_b).astype(BF16)


def _retention_tables():
    H, C = RET_HEADS, RET_CHUNK
    gamma = 1.0 - jnp.power(2.0, -5.0 - jnp.arange(H, dtype=F32))
    lg = jnp.log(gamma)
    idx = jnp.arange(C, dtype=F32)
    rel = idx[:, None] - idx[None, :]
    dmask = jnp.where(rel[None] >= 0, jnp.exp(jnp.maximum(rel, 0.0)[None] * lg[:, None, None]), 0.0)
    xi = jnp.exp((idx + 1.0)[None] * lg[:, None])
    zeta = jnp.exp((C - 1.0 - idx)[None] * lg[:, None])
    g_chunk = jnp.exp(C * lg)
    bc = lambda t: jnp.broadcast_to(t[:, :, None], (H, C, RET_HEAD_DIM)).astype(F32)
    g_rows = jnp.broadcast_to(g_chunk[:, None, None], (H, 1, RET_HEAD_DIM)).astype(F32)
    return dmask.astype(F32), bc(xi), bc(zeta), g_rows


def _branch_b(x, g, sh, sc, cosf, sinf, w, wg, tables, rg, wo):
    B, S, D = x.shape
    dr = wo.shape[0]
    tm = min(ROW_TILE, S)
    dmask, xi, zeta, g_chunk = tables
    vec = pl.BlockSpec((1, 1, D), lambda b, j: (b, 0, 0))
    rope_spec = pl.BlockSpec((1, tm, RET_HEAD_DIM), lambda b, j: (b, j, 0))
    return pl.pallas_call(
        _branch_b_kernel,
        grid=(B, S // tm),
        in_specs=[pl.BlockSpec((1, tm, D), lambda b, j: (b, j, 0)),
                  _const_spec((1, D)), vec, vec, rope_spec, rope_spec,
                  _const_spec(w.shape), _const_spec(wg.shape), _const_spec(dmask.shape),
                  _const_spec(xi.shape), _const_spec(zeta.shape), _const_spec(g_chunk.shape),
                  _const_spec(rg.shape), _const_spec(wo.shape)],
        out_specs=pl.BlockSpec((1, tm, D), lambda b, j: (b, j, 0)),
        out_shape=jax.ShapeDtypeStruct((B, S, D), BF16),
        scratch_shapes=[pltpu.VMEM((tm, dr), BF16), pltpu.VMEM((tm, dr), F32),
                        pltpu.VMEM((tm, dr), BF16), pltpu.VMEM((tm, dr), F32),
                        pltpu.VMEM((RET_HEADS, RET_HEAD_DIM, RET_HEAD_DIM), F32)],
        compiler_params=_params(2),
        name="branch_b",
    )(x, g, sh, sc, cosf, sinf, w, wg, dmask, xi, zeta, g_chunk, rg, wo)


def _out_router_kernel(ma_ref, mb_ref, mc_ref, x_ref, g1_ref, wo_ref, n2_ref, sh_ref, sc_ref,
                       rw_ref, rb_ref, x1_ref, idx_ref, prob_ref, rank_ref, cnt_ref, *rest):
    h2_refs, carry_ref = rest[:MOE_SPLIT], rest[MOE_SPLIT]
    tm = x_ref.shape[1]
    first = jnp.logical_and(pl.program_id(0) == 0, pl.program_id(1) == 0)

    @pl.when(first)
    def _():
        carry_ref[...] = jnp.zeros(carry_ref.shape, F32)

    m = ma_ref[0].astype(F32) + mb_ref[0].astype(F32) + mc_ref[0].astype(F32)
    x1 = x_ref[0] + g1_ref[0] * _bdot(m, wo_ref[...])
    x1_ref[0] = x1
    h2 = _norm_mod(x1, n2_ref[...], sh_ref[0], sc_ref[0])
    for h2_ref, part in zip(h2_refs, _pack_rows(h2)):
        h2_ref[...] = part

    rw = rw_ref[...]
    h_hi = h2.astype(BF16)
    h_lo = (h2 - h_hi.astype(F32)).astype(BF16)
    rw_hi = rw.astype(BF16)
    rw_lo = (rw - rw_hi.astype(F32)).astype(BF16)
    logits = (jnp.dot(h_hi, rw_hi, preferred_element_type=F32)
              + jnp.dot(h_hi, rw_lo, preferred_element_type=F32)
              + jnp.dot(h_lo, rw_hi, preferred_element_type=F32)) + rb_ref[...]
    lane = lax.broadcasted_iota(jnp.int32, logits.shape, 1)
    lane_f = lane.astype(F32)
    neg = jnp.float32(-jnp.inf)
    work = jnp.where(lane < N_EXPERTS, logits, neg)
    idx_out = jnp.zeros(logits.shape, jnp.int32)
    val_out = jnp.full(logits.shape, neg, F32)
    onehot_all = jnp.zeros(logits.shape, F32)
    sels = []
    for kk in range(TOP_K):
        mx = jnp.max(work, axis=-1, keepdims=True)
        first_idx = jnp.min(jnp.where(work == mx, lane_f, float(LANES)), axis=-1, keepdims=True)
        sel = lane_f == first_idx
        sels.append(sel)
        idx_out = jnp.where(lane == kk, first_idx.astype(jnp.int32), idx_out)
        val_out = jnp.where(lane == kk, mx, val_out)
        onehot_all = onehot_all + sel.astype(F32)
        work = jnp.where(sel, neg, work)

    ex = jnp.exp(val_out - val_out[:, 0:1])
    prob_ref[0] = ex / jnp.sum(ex, axis=-1, keepdims=True)
    idx_ref[0] = idx_out

    row = lax.broadcasted_iota(jnp.int32, (tm, tm), 0)
    col = lax.broadcasted_iota(jnp.int32, (tm, tm), 1)
    lower = (col < row).astype(BF16)
    before = jnp.dot(lower, onehot_all.astype(BF16), preferred_element_type=F32) + carry_ref[...]
    rank_out = jnp.zeros(logits.shape, jnp.int32)
    for kk in range(TOP_K):
        rk = jnp.sum(jnp.where(sels[kk], before, 0.0), axis=-1, keepdims=True)
        rank_out = jnp.where(lane == kk, rk.astype(jnp.int32), rank_out)
    rank_ref[0] = rank_out
    total = carry_ref[...] + jnp.sum(onehot_all, axis=0, keepdims=True)
    carry_ref[...] = total
    cnt_ref[...] = total


def _out_router(ma, mb, mc, x, g1, wo, n2, sh2, sc2, rw, rb):
    B, S, D = x.shape
    tm = min(ROW_TILE, S)
    tile = pl.BlockSpec((1, tm, D), lambda b, j: (b, j, 0))
    vec = pl.BlockSpec((1, 1, D), lambda b, j: (b, 0, 0))
    lanes = pl.BlockSpec((1, tm, LANES), lambda b, j: (b, j, 0))
    dq = D // (2 * MOE_SPLIT)
    per_seq = S // tm
    part = pl.BlockSpec((tm, dq), lambda b, j: (b * per_seq + j, 0))
    return pl.pallas_call(
        _out_router_kernel,
        grid=(B, S // tm),
        in_specs=[tile, tile, tile, tile, vec, _const_spec(wo.shape), _const_spec((1, D)),
                  vec, vec, _const_spec(rw.shape), _const_spec(rb.shape)],
        out_specs=[tile, lanes, lanes, lanes, pl.BlockSpec((1, LANES), lambda b, j: (0, 0))]
        + [part] * MOE_SPLIT,
        out_shape=[jax.ShapeDtypeStruct((B, S, D), F32),
                   jax.ShapeDtypeStruct((B, S, LANES), jnp.int32),
                   jax.ShapeDtypeStruct((B, S, LANES), F32),
                   jax.ShapeDtypeStruct((B, S, LANES), jnp.int32),
                   jax.ShapeDtypeStruct((1, LANES), F32)]
        + [jax.ShapeDtypeStruct((B * S, dq), jnp.uint32)] * MOE_SPLIT,
        scratch_shapes=[pltpu.VMEM((1, LANES), F32)],
        compiler_params=_params(2),
        name="out_router",
    )(ma, mb, mc, x, g1, wo, n2, sh2, sc2, rw, rb)


def _sc_row_gather(src, indices):
    n = indices.shape[0]
    d = src.shape[1]
    w = SC_GATHER_WINDOW
    mesh = plsc.VectorSubcoreMesh(core_axis_name="core", subcore_axis_name="subcore")
    per = n // w // SC_SUBCORES
    window = lambda i: (i % per) * SC_SUBCORES + i // per

    @pl.kernel(out_type=jax.ShapeDtypeStruct((n, d), src.dtype), mesh=mesh, scratch_types=[])
    def gather_kernel(src_hbm, idx_hbm, out_hbm):
        def body(idx_vmem, out_vmem):
            pltpu.sync_copy(src_hbm.at[idx_vmem.at[0]], out_vmem)

        pltpu.emit_pipeline(
            body,
            grid=(n // w,),
            in_specs=[pl.BlockSpec((1, w), index_map=lambda i: (0, window(i)))],
            out_specs=[pl.BlockSpec((w, d), index_map=lambda i: (window(i), 0))],
            core_axis_name=("core", "subcore"),
            dimension_semantics=(pltpu.PARALLEL,),
        )(idx_hbm, out_hbm)

    return gather_kernel(src, indices.reshape(1, n))


def _expert_kernel(be_ref, *refs):
    x_refs, refs = refs[:MOE_SPLIT], refs[MOE_SPLIT:]
    w1_ref, b1_ref, w2_ref, b2_ref = refs[:4]
    o_refs, (w1b_ref, w2b_ref) = refs[4:4 + MOE_SPLIT], refs[4 + MOE_SPLIT:]
    i = pl.program_id(0)
    f = w2_ref.shape[2]

    @pl.when(jnp.logical_or(i == 0, be_ref[i] != be_ref[jnp.maximum(i - 1, 0)]))
    def _():
        w1b_ref[...] = w1_ref[0, 0].astype(BF16)
        w2b_ref[...] = w2_ref[0, 0].astype(BF16)

    xb = _unpack_rows([x_ref[...] for x_ref in x_refs]).astype(BF16)
    gu = jnp.dot(xb, w1b_ref[...], preferred_element_type=F32) + b1_ref[0, 0]
    glu = jnp.minimum(gu[:, :f], SWIGLU_LIMIT)
    lin = jnp.clip(gu[:, f:], -SWIGLU_LIMIT, SWIGLU_LIMIT)
    act = glu * _sigmoid(SWIGLU_ALPHA * glu) * (lin + 1.0)
    y = _bdot(act, w2b_ref[...]) + b2_ref[0, 0]
    for o_ref, part in zip(o_refs, _pack_rows(y)):
        o_ref[...] = part


def _experts(xparts, blk_expert, w1, b1, w2, b2, layer):
    rows, dq = xparts[0].shape
    L, E, D, F2 = w1.shape
    nb = blk_expert.shape[0]
    blk = EXPERT_BLOCK
    per_expert = lambda shape: pl.BlockSpec((1, 1) + shape, lambda i, be: (layer, be[i], 0, 0))
    grid_spec = pltpu.PrefetchScalarGridSpec(
        num_scalar_prefetch=1,
        grid=(nb,),
        in_specs=[pl.BlockSpec((blk, dq), lambda i, be: (i, 0))] * MOE_SPLIT
        + [per_expert((D, F2)), per_expert((1, F2)), per_expert((F2 // 2, D)), per_expert((1, D))],
        out_specs=[pl.BlockSpec((blk, dq), lambda i, be: (i, 0))] * MOE_SPLIT,
        scratch_shapes=[pltpu.VMEM((D, F2), BF16), pltpu.VMEM((F2 // 2, D), BF16)])
    return pl.pallas_call(
        _expert_kernel,
        grid_spec=grid_spec,
        out_shape=[jax.ShapeDtypeStruct((rows, dq), jnp.uint32)] * MOE_SPLIT,
        compiler_params=_params(1),
        name="experts",
    )(blk_expert, *xparts, w1, b1.reshape(L, E, 1, F2), w2, b2.reshape(L, E, 1, D))


def _combine_kernel(*refs, final_norm):
    y_refs, (x1_ref, g2_ref, prob_ref, fg_ref, o_ref) = refs[:MOE_SPLIT], refs[MOE_SPLIT:]
    tm = x1_ref.shape[0]
    prob = prob_ref[...]
    acc = jnp.zeros(x1_ref.shape, F32)
    for kk in range(TOP_K):
        rows = slice(kk * tm, (kk + 1) * tm)
        acc = acc + prob[:, kk:kk + 1] * _unpack_rows([y_ref[rows, :] for y_ref in y_refs])
    x2 = x1_ref[...] + g2_ref[0] * acc
    if final_norm:
        x2 = x2 * lax.rsqrt(jnp.mean(x2 * x2, axis=-1, keepdims=True) + EPS) * fg_ref[...]
    o_ref[...] = x2


def _combine(yparts, x1, g2, prob, final_g, seq_len, final_norm):
    N, D = x1.shape
    tm = min(COMBINE_TILE, seq_len)
    per_seq = seq_len // tm
    return pl.pallas_call(
        functools.partial(_combine_kernel, final_norm=final_norm),
        grid=(N // tm,),
        in_specs=[pl.BlockSpec((TOP_K * tm, D // (2 * MOE_SPLIT)), lambda i: (i, 0))] * MOE_SPLIT
        + [pl.BlockSpec((tm, D), lambda i: (i, 0)),
                  pl.BlockSpec((1, 1, D), lambda i: (i // per_seq, 0, 0)),
                  pl.BlockSpec((tm, LANES), lambda i: (i, 0)),
                  pl.BlockSpec((1, D), lambda i: (0, 0))],
        out_specs=pl.BlockSpec((tm, D), lambda i: (i, 0)),
        out_shape=jax.ShapeDtypeStruct((N, D), F32),
        compiler_params=_params(1),
        name="combine",
    )(*yparts, x1, g2, prob, final_g)


def _routing_tables(counts, idx, rank, n_tokens):
    blk = EXPERT_BLOCK
    A = n_tokens * TOP_K
    nb = (A + N_EXPERTS * (blk - 1)) // blk
    group = max(1, SC_GATHER_ROWS // blk)
    nb = -(-nb // group) * group
    counts = counts.astype(jnp.int32)
    padded = ((counts + blk - 1) // blk) * blk
    pends = jnp.cumsum(padded)
    pstarts = pends - padded
    block_start = jnp.arange(nb, dtype=jnp.int32) * blk
    blk_expert = jnp.minimum(jnp.sum(block_start[:, None] >= pends[None, :], axis=1),
                             N_EXPERTS - 1).astype(jnp.int32)
    nwin = padded // SC_GATHER_WINDOW
    def offset(r, e):
        return (r % SC_GATHER_WINDOW) * nwin[e] + r // SC_GATHER_WINDOW
    dest = pstarts[idx] + offset(rank, idx)
    tok = jnp.repeat(jnp.arange(n_tokens, dtype=jnp.int32), TOP_K)
    n_pad = nb * blk - A
    pad_per = padded - counts
    padcum = jnp.cumsum(pad_per)
    q = jnp.arange(n_pad, dtype=jnp.int32)
    e_q = jnp.sum(q[:, None] >= padcum[None, :], axis=1).astype(jnp.int32)
    e_c = jnp.minimum(e_q, N_EXPERTS - 1)
    key_in = pstarts[e_c] + offset(counts[e_c] + q - (padcum - pad_per)[e_c], e_c)
    key_tail = pends[-1] - padcum[-1] + q
    pad_keys = jnp.where(e_q < N_EXPERTS, key_in, key_tail)
    keys = jnp.concatenate([dest.reshape(A), pad_keys])
    vals = jnp.concatenate([tok, pad_keys % n_tokens])
    _, tok_of_slot = lax.sort((keys, vals), num_keys=1)
    return dest, tok_of_slot, blk_expert


def kernel(x, c, positions, mod_w, mod_b, norm1_g, w_in, conv_a_w, w_a_out, ret_norm_g, w_b_out, glu_b, conv_c_w, conv_c_b, ln_c_g, ln_c_b, w_c_out, w_o, norm2_g, router_w, router_b, exp_w1, exp_b1, exp_w2, exp_b2, final_g):
    B, S, D = x.shape
    L = mod_w.shape[0]
    N = B * S
    da = w_a_out.shape[1]
    dr = w_b_out.shape[1]
    dc = w_c_out.shape[1]
    s1 = 3 * da
    s2 = s1 + 4 * dr
    s3 = s2 + 2 * dc

    mod = _modulation(c, mod_w, mod_b)
    cosf, sinf = _rope_tables(positions)
    tables = _retention_tables()
    row = lambda v: v.reshape(1, -1)
    rw_pad = jnp.zeros((L, D, LANES), F32).at[:, :, :N_EXPERTS].set(router_w)
    rb_pad = jnp.zeros((L, 1, LANES), F32).at[:, 0, :N_EXPERTS].set(router_b)

    for l in range(L):
        sh1, sc1, g1, sh2, sc2, g2 = [mod[l, :, i * D:(i + 1) * D].reshape(B, 1, D) for i in range(6)]
        wl = w_in[l].astype(BF16)
        wg = wl[:, s3:]
        n1 = row(norm1_g[l])
        ma = _branch_a(x, n1, sh1, sc1, wl[:, :s1], wg[:, :D], conv_a_w[l],
                       w_a_out[l].astype(BF16))
        mb = _branch_b(x, n1, sh1, sc1, cosf, sinf, wl[:, s1:s2], wg[:, D:2 * D], tables,
                       row(ret_norm_g[l]), w_b_out[l].astype(BF16))
        mc = _branch_c(x, n1, sh1, sc1, wl[:, s2:s3], wg[:, 2 * D:], row(glu_b[l]), conv_c_w[l],
                       row(conv_c_b[l]), row(ln_c_g[l]), row(ln_c_b[l]), w_c_out[l].astype(BF16))
        x1, idx, prob, rank, counts, *h2_parts = _out_router(
            ma, mb, mc, x, g1, w_o[l].astype(BF16), row(norm2_g[l]), sh2, sc2, rw_pad[l], rb_pad[l])
        idx = idx.reshape(N, LANES)[:, :TOP_K]
        rank = rank.reshape(N, LANES)[:, :TOP_K]
        dest, tok_of_slot, blk_expert = _routing_tables(counts[0, :N_EXPERTS], idx, rank, N)
        xparts = [_sc_row_gather(p, tok_of_slot) for p in h2_parts]
        yparts = _experts(xparts, blk_expert, exp_w1, exp_b1, exp_w2, exp_b2, layer=l)
        tmc = min(COMBINE_TILE, S)
        dest_kmajor = dest.reshape(N // tmc, tmc, TOP_K).transpose(0, 2, 1).reshape(N * TOP_K)
        ycat = [_sc_row_gather(p, dest_kmajor) for p in yparts]
        x = _combine(ycat, x1.reshape(N, D), g2, prob.reshape(N, LANES), row(final_g),
                     S, final_norm=(l == L - 1)).reshape(B, S, D)
    return x
```

```python
import functools

import jax
import jax.numpy as jnp
from jax import lax
from jax.experimental import pallas as pl
from jax.experimental.pallas import tpu as pltpu
from jax.experimental.pallas import tpu_sc as plsc

F32 = jnp.float32
BF16 = jnp.bfloat16

EPS = 1e-6
SHORT_CONV_W = 3
RET_HEADS = 8
RET_HEAD_DIM = 128
RET_CHUNK = 128
ROPE_BASE = 10000.0
CONF_KERNEL = 31
N_EXPERTS = 32
TOP_K = 4
SWIGLU_LIMIT = 7.0
SWIGLU_ALPHA = 1.702

LANES = 128
SUBLANES = 8
VMEM_LIMIT = 56 * 1024 * 1024

ROW_TILE = 512
EXPERT_BLOCK = 512
SC_GATHER_WINDOW = 128
MOE_SPLIT = 2
SC_SUBCORES = 32
SC_GATHER_ROWS = SC_GATHER_WINDOW * SC_SUBCORES
COMBINE_TILE = 512
CONV_ROWS = 32
CONF_HALO = 32


def _params(n_grid):
    return pltpu.CompilerParams(
        dimension_semantics=("arbitrary",) * n_grid, vmem_limit_bytes=VMEM_LIMIT)


def _const_spec(shape):
    nd = len(shape)
    return pl.BlockSpec(shape, lambda *_: (0,) * nd, pipeline_mode=pl.Buffered(1))


def _sigmoid(x):
    return 1.0 / (1.0 + jnp.exp(-x))


def _silu(x):
    return x * _sigmoid(x)


def _norm_mod(x, g, sh, sc):
    y = x * lax.rsqrt(jnp.mean(x * x, axis=-1, keepdims=True) + EPS)
    return (y * g) * (1.0 + sc) + sh


def _bdot(a, b):
    return jnp.dot(a.astype(BF16), b.astype(BF16), preferred_element_type=F32)


def _pack_rows(v):
    c = v.shape[1] // MOE_SPLIT
    parts = []
    for p in range(MOE_SPLIT):
        lo = v[:, p * c:p * c + c // 2].astype(BF16).astype(F32)
        hi = v[:, p * c + c // 2:(p + 1) * c].astype(BF16).astype(F32)
        lo_bits = lax.bitcast_convert_type(lo, jnp.uint32) >> 16
        hi_bits = lax.bitcast_convert_type(hi, jnp.uint32) & jnp.uint32(0xFFFF0000)
        parts.append(hi_bits | lo_bits)
    return parts


def _unpack_rows(parts):
    cols = []
    for w in parts:
        cols.append(lax.bitcast_convert_type(w << 16, F32))
        cols.append(lax.bitcast_convert_type(w & jnp.uint32(0xFFFF0000), F32))
    return jnp.concatenate(cols, axis=1)


def _mod_kernel(c_ref, w_ref, b_ref, o_ref):
    cond = _silu(c_ref[...])
    o_ref[0] = jnp.dot(cond, w_ref[0], preferred_element_type=F32,
                       precision=lax.Precision.HIGHEST) + b_ref[0]


def _modulation(c, mod_w, mod_b):
    L, D, M = mod_w.shape
    B = c.shape[0]
    tn = 1024
    return pl.pallas_call(
        _mod_kernel,
        grid=(L, M // tn),
        in_specs=[pl.BlockSpec((B, D), lambda l, j: (0, 0)),
                  pl.BlockSpec((1, D, tn), lambda l, j: (l, 0, j)),
                  pl.BlockSpec((1, 1, tn), lambda l, j: (l, 0, j))],
        out_specs=pl.BlockSpec((1, B, tn), lambda l, j: (l, 0, j)),
        out_shape=jax.ShapeDtypeStruct((L, B, M), F32),
        compiler_params=_params(2),
        name="modulation",
    )(c, mod_w, mod_b.reshape(L, 1, M))


def _rope_kernel(pos_ref, invf_ref, cos_ref, sin_ref):
    ang = pos_ref[0].astype(F32) * invf_ref[...]
    lane = lax.broadcasted_iota(jnp.int32, ang.shape, 1)
    s = jnp.sin(ang)
    cos_ref[0] = jnp.cos(ang)
    sin_ref[0] = jnp.where(lane < RET_HEAD_DIM // 2, -s, s)


def _rope_tables(positions):
    B, S = positions.shape
    inv_freq = 1.0 / (ROPE_BASE ** (jnp.arange(0, RET_HEAD_DIM, 2, dtype=F32) / RET_HEAD_DIM))
    invf = jnp.concatenate([inv_freq, inv_freq]).reshape(1, RET_HEAD_DIM)
    ts = min(S, 1024)
    spec = pl.BlockSpec((1, ts, RET_HEAD_DIM), lambda b, j: (b, j, 0))
    return pl.pallas_call(
        _rope_kernel,
        grid=(B, S // ts),
        in_specs=[pl.BlockSpec((1, ts, 1), lambda b, j: (b, j, 0)),
                  pl.BlockSpec((1, RET_HEAD_DIM), lambda b, j: (0, 0))],
        out_specs=[spec, spec],
        out_shape=[jax.ShapeDtypeStruct((B, S, RET_HEAD_DIM), F32)] * 2,
        compiler_params=_params(2),
        name="rope_tables",
    )(positions.reshape(B, S, 1), invf)


def _branch_a_kernel(x_ref, g_ref, sh_ref, sc_ref, w_ref, wg_ref, cw_ref, wo_ref, o_ref, buf_ref):
    tm = x_ref.shape[1]
    da = wo_ref.shape[0]

    @pl.when(pl.program_id(1) == 0)
    def _():
        buf_ref[0:SUBLANES, :] = jnp.zeros((SUBLANES, da), F32)

    h = _norm_mod(x_ref[0], g_ref[...], sh_ref[0], sc_ref[0]).astype(BF16)
    p = jnp.dot(h, w_ref[...], preferred_element_type=F32)
    b_g, c_g, xa = p[:, :da], p[:, da:2 * da], p[:, 2 * da:]
    u = c_g * xa
    buf_ref[SUBLANES:SUBLANES + tm, :] = u
    conv = cw_ref[2:3, :] * u
    for j in range(SHORT_CONV_W - 1):
        off = SUBLANES - (SHORT_CONV_W - 1) + j
        conv = conv + cw_ref[j:j + 1, :] * buf_ref[off:off + tm, :]
    buf_ref[0:SUBLANES, :] = buf_ref[tm:tm + SUBLANES, :]
    y_a = _bdot(b_g * conv, wo_ref[...])
    gate = _sigmoid(jnp.dot(h, wg_ref[...], preferred_element_type=F32))
    o_ref[0] = (gate * y_a).astype(BF16)


def _branch_a(x, g, sh, sc, w, wg, cw, wo):
    B, S, D = x.shape
    da = wo.shape[0]
    tm = min(ROW_TILE, S)
    vec = pl.BlockSpec((1, 1, D), lambda b, j: (b, 0, 0))
    return pl.pallas_call(
        _branch_a_kernel,
        grid=(B, S // tm),
        in_specs=[pl.BlockSpec((1, tm, D), lambda b, j: (b, j, 0)),
                  _const_spec((1, D)), vec, vec,
                  _const_spec(w.shape), _const_spec(wg.shape), _const_spec(cw.shape),
                  _const_spec(wo.shape)],
        out_specs=pl.BlockSpec((1, tm, D), lambda b, j: (b, j, 0)),
        out_shape=jax.ShapeDtypeStruct((B, S, D), BF16),
        scratch_shapes=[pltpu.VMEM((tm + SUBLANES, da), F32)],
        compiler_params=_params(2),
        name="branch_a",
    )(x, g, sh, sc, w, wg, cw, wo)


def _branch_c_kernel(x_ref, g_ref, sh_ref, sc_ref, w_ref, wg_ref, glub_ref, cw_ref, cb_ref,
                     lng_ref, lnb_ref, wo_ref, o_ref, buf_ref, conv_ref):
    tm = x_ref.shape[1]
    dc = wo_ref.shape[0]

    @pl.when(pl.program_id(1) == 0)
    def _():
        buf_ref[0, 0:CONF_HALO, :] = jnp.zeros((CONF_HALO, dc), F32)

    h = _norm_mod(x_ref[0], g_ref[...], sh_ref[0], sc_ref[0]).astype(BF16)
    p = jnp.dot(h, w_ref[...], preferred_element_type=F32) + glub_ref[...]
    buf_ref[0, CONF_HALO:CONF_HALO + tm, :] = p[:, :dc] * _sigmoid(p[:, dc:])
    for b in range(1, SUBLANES):
        buf_ref[b, SUBLANES:, :] = buf_ref[0, SUBLANES - b:tm + CONF_HALO - b, :]

    base = CONF_HALO - (CONF_KERNEL - 1)

    def strip(s, carry):
        r0 = pl.multiple_of(s * CONV_ROWS, CONV_ROWS)
        acc = jnp.zeros((CONV_ROWS, dc), F32)
        for j in range(CONF_KERNEL):
            a = -(-(base + j) // SUBLANES)
            b = a * SUBLANES - (base + j)
            acc = acc + cw_ref[j:j + 1, :] * buf_ref[b, pl.ds(r0 + a * SUBLANES, CONV_ROWS), :]
        conv_ref[pl.ds(r0, CONV_ROWS), :] = acc
        return carry

    lax.fori_loop(0, tm // CONV_ROWS, strip, 0)
    buf_ref[0, 0:CONF_HALO, :] = buf_ref[0, tm:tm + CONF_HALO, :]

    u = conv_ref[...] + cb_ref[...]
    mu = jnp.mean(u, axis=-1, keepdims=True)
    d = u - mu
    var = jnp.mean(d * d, axis=-1, keepdims=True)
    y = d * lax.rsqrt(var + EPS) * lng_ref[...] + lnb_ref[...]
    y_c = _bdot(_silu(y), wo_ref[...])
    gate = _sigmoid(jnp.dot(h, wg_ref[...], preferred_element_type=F32))
    o_ref[0] = (gate * y_c).astype(BF16)


def _branch_c(x, g, sh, sc, w, wg, glub, cw, cb, lng, lnb, wo):
    B, S, D = x.shape
    dc = wo.shape[0]
    tm = min(ROW_TILE, S)
    vec = pl.BlockSpec((1, 1, D), lambda b, j: (b, 0, 0))
    return pl.pallas_call(
        _branch_c_kernel,
        grid=(B, S // tm),
        in_specs=[pl.BlockSpec((1, tm, D), lambda b, j: (b, j, 0)),
                  _const_spec((1, D)), vec, vec,
                  _const_spec(w.shape), _const_spec(wg.shape), _const_spec(glub.shape),
                  _const_spec(cw.shape), _const_spec(cb.shape), _const_spec(lng.shape),
                  _const_spec(lnb.shape), _const_spec(wo.shape)],
        out_specs=pl.BlockSpec((1, tm, D), lambda b, j: (b, j, 0)),
        out_shape=jax.ShapeDtypeStruct((B, S, D), BF16),
        scratch_shapes=[pltpu.VMEM((SUBLANES, tm + CONF_HALO, dc), F32), pltpu.VMEM((tm, dc), F32)],
        compiler_params=_params(2),
        name="branch_c",
    )(x, g, sh, sc, w, wg, glub, cw, cb, lng, lnb, wo)


def _branch_b_kernel(x_ref, g_ref, sh_ref, sc_ref, cos_ref, sin_ref, w_ref, wg_ref,
                     dmask_ref, xi_ref, zeta_ref, gch_ref, rg_ref, wo_ref, o_ref,
                     q_ref, k_ref, v_ref, r_ref, state_ref):
    tm = x_ref.shape[1]
    dr = wo_ref.shape[0]
    hd = RET_HEAD_DIM

    @pl.when(pl.program_id(1) == 0)
    def _():
        state_ref[...] = jnp.zeros(state_ref.shape, F32)

    h = _norm_mod(x_ref[0], g_ref[...], sh_ref[0], sc_ref[0]).astype(BF16)
    cos = cos_ref[0]
    sin = sin_ref[0]
    scale = hd ** -0.5

    def rope(t):
        return t * cos + pltpu.roll(t, hd // 2, axis=1) * sin

    q = jnp.dot(h, w_ref[:, 0:dr], preferred_element_type=F32) * scale
    k = jnp.dot(h, w_ref[:, dr:2 * dr], preferred_element_type=F32)
    for hh in range(RET_HEADS):
        sl = slice(hh * hd, (hh + 1) * hd)
        q_ref[:, sl] = rope(q[:, sl]).astype(BF16)
        k_ref[:, sl] = rope(k[:, sl])
    v_ref[...] = jnp.dot(h, w_ref[:, 2 * dr:3 * dr], preferred_element_type=F32).astype(BF16)

    for ci in range(tm // RET_CHUNK):
        rows = pl.ds(ci * RET_CHUNK, RET_CHUNK)
        for hh in range(RET_HEADS):
            sl = slice(hh * hd, (hh + 1) * hd)
            qh = q_ref[rows, sl]
            kh = k_ref[rows, sl]
            vh = v_ref[rows, sl]
            st = state_ref[hh]
            scores = lax.dot_general(qh, kh.astype(BF16), (((1,), (1,)), ((), ())),
                                     preferred_element_type=F32) * dmask_ref[hh]
            inner = _bdot(scores, vh)
            cross = _bdot(qh, st) * xi_ref[hh]
            kz = (kh * zeta_ref[hh]).astype(BF16)
            state_ref[hh] = st * gch_ref[hh] + lax.dot_general(
                kz, vh, (((0,), (0,)), ((), ())), preferred_element_type=F32)
            o = inner + cross
            mu = jnp.mean(o, axis=-1, keepdims=True)
            d = o - mu
            var = jnp.mean(d * d, axis=-1, keepdims=True)
            r_ref[rows, sl] = d * lax.rsqrt(var + EPS)

    og = jnp.dot(h, w_ref[:, 3 * dr:4 * dr], preferred_element_type=F32)
    y_b = _bdot(_silu(og) * (r_ref[...] * rg_ref[...]), wo_ref[...])
    gate = _sigmoid(jnp.dot(h, wg_ref[...], preferred_element_type=F32))
    o_ref[0] = (gate * y_b).astype(BF16)


def _retention_tables():
    H, C = RET_HEADS, RET_CHUNK
    gamma = 1.0 - jnp.power(2.0, -5.0 - jnp.arange(H, dtype=F32))
    lg = jnp.log(gamma)
    idx = jnp.arange(C, dtype=F32)
    rel = idx[:, None] - idx[None, :]
    dmask = jnp.where(rel[None] >= 0, jnp.exp(jnp.maximum(rel, 0.0)[None] * lg[:, None, None]), 0.0)
    xi = jnp.exp((idx + 1.0)[None] * lg[:, None])
    zeta = jnp.exp((C - 1.0 - idx)[None] * lg[:, None])
    g_chunk = jnp.exp(C * lg)
    bc = lambda t: jnp.broadcast_to(t[:, :, None], (H, C, RET_HEAD_DIM)).astype(F32)
    g_rows = jnp.broadcast_to(g_chunk[:, None, None], (H, 1, RET_HEAD_DIM)).astype(F32)
    return dmask.astype(F32), bc(xi), bc(zeta), g_rows


def _branch_b(x, g, sh, sc, cosf, sinf, w, wg, tables, rg, wo):
    B, S, D = x.shape
    dr = wo.shape[0]
    tm = min(ROW_TILE, S)
    dmask, xi, zeta, g_chunk = tables
    vec = pl.BlockSpec((1, 1, D), lambda b, j: (b, 0, 0))
    rope_spec = pl.BlockSpec((1, tm, RET_HEAD_DIM), lambda b, j: (b, j, 0))
    return pl.pallas_call(
        _branch_b_kernel,
        grid=(B, S // tm),
        in_specs=[pl.BlockSpec((1, tm, D), lambda b, j: (b, j, 0)),
                  _const_spec((1, D)), vec, vec, rope_spec, rope_spec,
                  _const_spec(w.shape), _const_spec(wg.shape), _const_spec(dmask.shape),
                  _const_spec(xi.shape), _const_spec(zeta.shape), _const_spec(g_chunk.shape),
                  _const_spec(rg.shape), _const_spec(wo.shape)],
        out_specs=pl.BlockSpec((1, tm, D), lambda b, j: (b, j, 0)),
        out_shape=jax.ShapeDtypeStruct((B, S, D), BF16),
        scratch_shapes=[pltpu.VMEM((tm, dr), BF16), pltpu.VMEM((tm, dr), F32),
                        pltpu.VMEM((tm, dr), BF16), pltpu.VMEM((tm, dr), F32),
                        pltpu.VMEM((RET_HEADS, RET_HEAD_DIM, RET_HEAD_DIM), F32)],
        compiler_params=_params(2),
        name="branch_b",
    )(x, g, sh, sc, cosf, sinf, w, wg, dmask, xi, zeta, g_chunk, rg, wo)


def _out_router_kernel(ma_ref, mb_ref, mc_ref, x_ref, g1_ref, wo_ref, n2_ref, sh_ref, sc_ref,
                       rw_ref, rb_ref, x1_ref, idx_ref, prob_ref, rank_ref, cnt_ref, *rest):
    h2_refs, carry_ref = rest[:MOE_SPLIT], rest[MOE_SPLIT]
    tm = x_ref.shape[1]
    first = jnp.logical_and(pl.program_id(0) == 0, pl.program_id(1) == 0)

    @pl.when(first)
    def _():
        carry_ref[...] = jnp.zeros(carry_ref.shape, F32)

    m = ma_ref[0].astype(F32) + mb_ref[0].astype(F32) + mc_ref[0].astype(F32)
    x1 = x_ref[0] + g1_ref[0] * _bdot(m, wo_ref[...])
    x1_ref[0] = x1
    h2 = _norm_mod(x1, n2_ref[...], sh_ref[0], sc_ref[0])
    for h2_ref, part in zip(h2_refs, _pack_rows(h2)):
        h2_ref[...] = part

    rw = rw_ref[...]
    h_hi = h2.astype(BF16)
    h_lo = (h2 - h_hi.astype(F32)).astype(BF16)
    rw_hi = rw.astype(BF16)
    rw_lo = (rw - rw_hi.astype(F32)).astype(BF16)
    logits = (jnp.dot(h_hi, rw_hi, preferred_element_type=F32)
              + jnp.dot(h_hi, rw_lo, preferred_element_type=F32)
              + jnp.dot(h_lo, rw_hi, preferred_element_type=F32)) + rb_ref[...]
    lane = lax.broadcasted_iota(jnp.int32, logits.shape, 1)
    lane_f = lane.astype(F32)
    neg = jnp.float32(-jnp.inf)
    work = jnp.where(lane < N_EXPERTS, logits, neg)
    idx_out = jnp.zeros(logits.shape, jnp.int32)
    val_out = jnp.full(logits.shape, neg, F32)
    onehot_all = jnp.zeros(logits.shape, F32)
    sels = []
    for kk in range(TOP_K):
        mx = jnp.max(work, axis=-1, keepdims=True)
        first_idx = jnp.min(jnp.where(work == mx, lane_f, float(LANES)), axis=-1, keepdims=True)
        sel = lane_f == first_idx
        sels.append(sel)
        idx_out = jnp.where(lane == kk, first_idx.astype(jnp.int32), idx_out)
        val_out = jnp.where(lane == kk, mx, val_out)
        onehot_all = onehot_all + sel.astype(F32)
        work = jnp.where(sel, neg, work)

    ex = jnp.exp(val_out - val_out[:, 0:1])
    prob_ref[0] = ex / jnp.sum(ex, axis=-1, keepdims=True)
    idx_ref[0] = idx_out

    row = lax.broadcasted_iota(jnp.int32, (tm, tm), 0)
    col = lax.broadcasted_iota(jnp.int32, (tm, tm), 1)
    lower = (col < row).astype(BF16)
    before = jnp.dot(lower, onehot_all.astype(BF16), preferred_element_type=F32) + carry_ref[...]
    rank_out = jnp.zeros(logits.shape, jnp.int32)
    for kk in range(TOP_K):
        rk = jnp.sum(jnp.where(sels[kk], before, 0.0), axis=-1, keepdims=True)
        rank_out = jnp.where(lane == kk, rk.astype(jnp.int32), rank_out)
    rank_ref[0] = rank_out
    total = carry_ref[...] + jnp.sum(onehot_all, axis=0, keepdims=True)
    carry_ref[...] = total
    cnt_ref[...] = total


def _out_router(ma, mb, mc, x, g1, wo, n2, sh2, sc2, rw, rb):
    B, S, D = x.shape
    tm = min(ROW_TILE, S)
    tile = pl.BlockSpec((1, tm, D), lambda b, j: (b, j, 0))
    vec = pl.BlockSpec((1, 1, D), lambda b, j: (b, 0, 0))
    lanes = pl.BlockSpec((1, tm, LANES), lambda b, j: (b, j, 0))
    dq = D // (2 * MOE_SPLIT)
    per_seq = S // tm
    part = pl.BlockSpec((tm, dq), lambda b, j: (b * per_seq + j, 0))
    return pl.pallas_call(
        _out_router_kernel,
        grid=(B, S // tm),
        in_specs=[tile, tile, tile, tile, vec, _const_spec(wo.shape), _const_spec((1, D)),
                  vec, vec, _const_spec(rw.shape), _const_spec(rb.shape)],
        out_specs=[tile, lanes, lanes, lanes, pl.BlockSpec((1, LANES), lambda b, j: (0, 0))]
        + [part] * MOE_SPLIT,
        out_shape=[jax.ShapeDtypeStruct((B, S, D), F32),
                   jax.ShapeDtypeStruct((B, S, LANES), jnp.int32),
                   jax.ShapeDtypeStruct((B, S, LANES), F32),
                   jax.ShapeDtypeStruct((B, S, LANES), jnp.int32),
                   jax.ShapeDtypeStruct((1, LANES), F32)]
        + [jax.ShapeDtypeStruct((B * S, dq), jnp.uint32)] * MOE_SPLIT,
        scratch_shapes=[pltpu.VMEM((1, LANES), F32)],
        compiler_params=_params(2),
        name="out_router",
    )(ma, mb, mc, x, g1, wo, n2, sh2, sc2, rw, rb)


def _sc_row_gather(src, indices):
    n = indices.shape[0]
    d = src.shape[1]
    w = SC_GATHER_WINDOW
    mesh = plsc.VectorSubcoreMesh(core_axis_name="core", subcore_axis_name="subcore")
    per = n // w // SC_SUBCORES
    window = lambda i: (i % per) * SC_SUBCORES + i // per

    @pl.kernel(out_type=jax.ShapeDtypeStruct((n, d), src.dtype), mesh=mesh, scratch_types=[])
    def gather_kernel(src_hbm, idx_hbm, out_hbm):
        def body(idx_vmem, out_vmem):
            pltpu.sync_copy(src_hbm.at[idx_vmem.at[0]], out_vmem)

        pltpu.emit_pipeline(
            body,
            grid=(n // w,),
            in_specs=[pl.BlockSpec((1, w), index_map=lambda i: (0, window(i)))],
            out_specs=[pl.BlockSpec((w, d), index_map=lambda i: (window(i), 0))],
            core_axis_name=("core", "subcore"),
            dimension_semantics=(pltpu.PARALLEL,),
        )(idx_hbm, out_hbm)

    return gather_kernel(src, indices.reshape(1, n))


def _expert_kernel(be_ref, *refs):
    x_refs, refs = refs[:MOE_SPLIT], refs[MOE_SPLIT:]
    w1_ref, b1_ref, w2_ref, b2_ref = refs[:4]
    o_refs, (w1b_ref, w2b_ref) = refs[4:4 + MOE_SPLIT], refs[4 + MOE_SPLIT:]
    i = pl.program_id(0)
    f = w2_ref.shape[2]

    @pl.when(jnp.logical_or(i == 0, be_ref[i] != be_ref[jnp.maximum(i - 1, 0)]))
    def _():
        w1b_ref[...] = w1_ref[0, 0].astype(BF16)
        w2b_ref[...] = w2_ref[0, 0].astype(BF16)

    xb = _unpack_rows([x_ref[...] for x_ref in x_refs]).astype(BF16)
    gu = jnp.dot(xb, w1b_ref[...], preferred_element_type=F32) + b1_ref[0, 0]
    glu = jnp.minimum(gu[:, :f], SWIGLU_LIMIT)
    lin = jnp.clip(gu[:, f:], -SWIGLU_LIMIT, SWIGLU_LIMIT)
    act = glu * _sigmoid(SWIGLU_ALPHA * glu) * (lin + 1.0)
    y = _bdot(act, w2b_ref[...]) + b2_ref[0, 0]
    for o_ref, part in zip(o_refs, _pack_rows(y)):
        o_ref[...] = part


def _experts(xparts, blk_expert, w1, b1, w2, b2, layer):
    rows, dq = xparts[0].shape
    L, E, D, F2 = w1.shape
    nb = blk_expert.shape[0]
    blk = EXPERT_BLOCK
    per_expert = lambda shape: pl.BlockSpec((1, 1) + shape, lambda i, be: (layer, be[i], 0, 0))
    grid_spec = pltpu.PrefetchScalarGridSpec(
        num_scalar_prefetch=1,
        grid=(nb,),
        in_specs=[pl.BlockSpec((blk, dq), lambda i, be: (i, 0))] * MOE_SPLIT
        + [per_expert((D, F2)), per_expert((1, F2)), per_expert((F2 // 2, D)), per_expert((1, D))],
        out_specs=[pl.BlockSpec((blk, dq), lambda i, be: (i, 0))] * MOE_SPLIT,
        scratch_shapes=[pltpu.VMEM((D, F2), BF16), pltpu.VMEM((F2 // 2, D), BF16)])
    return pl.pallas_call(
        _expert_kernel,
        grid_spec=grid_spec,
        out_shape=[jax.ShapeDtypeStruct((rows, dq), jnp.uint32)] * MOE_SPLIT,
        compiler_params=_params(1),
        name="experts",
    )(blk_expert, *xparts, w1, b1.reshape(L, E, 1, F2), w2, b2.reshape(L, E, 1, D))


def _combine_kernel(*refs, final_norm):
    y_refs, (x1_ref, g2_ref, prob_ref, fg_ref, o_ref) = refs[:MOE_SPLIT], refs[MOE_SPLIT:]
    tm = x1_ref.shape[0]
    prob = prob_ref[...]
    acc = jnp.zeros(x1_ref.shape, F32)
    for kk in range(TOP_K):
        rows = slice(kk * tm, (kk + 1) * tm)
        acc = acc + prob[:, kk:kk + 1] * _unpack_rows([y_ref[rows, :] for y_ref in y_refs])
    x2 = x1_ref[...] + g2_ref[0] * acc
    if final_norm:
        x2 = x2 * lax.rsqrt(jnp.mean(x2 * x2, axis=-1, keepdims=True) + EPS) * fg_ref[...]
    o_ref[...] = x2


def _combine(yparts, x1, g2, prob, final_g, seq_len, final_norm):
    N, D = x1.shape
    tm = min(COMBINE_TILE, seq_len)
    per_seq = seq_len // tm
    return pl.pallas_call(
        functools.partial(_combine_kernel, final_norm=final_norm),
        grid=(N // tm,),
        in_specs=[pl.BlockSpec((TOP_K * tm, D // (2 * MOE_SPLIT)), lambda i: (i, 0))] * MOE_SPLIT
        + [pl.BlockSpec((tm, D), lambda i: (i, 0)),
                  pl.BlockSpec((1, 1, D), lambda i: (i // per_seq, 0, 0)),
                  pl.BlockSpec((tm, LANES), lambda i: (i, 0)),
                  pl.BlockSpec((1, D), lambda i: (0, 0))],
        out_specs=pl.BlockSpec((tm, D), lambda i: (i, 0)),
        out_shape=jax.ShapeDtypeStruct((N, D), F32),
        compiler_params=_params(1),
        name="combine",
    )(*yparts, x1, g2, prob, final_g)


def _routing_tables(counts, idx, rank, n_tokens):
    blk = EXPERT_BLOCK
    A = n_tokens * TOP_K
    nb = (A + N_EXPERTS * (blk - 1)) // blk
    group = max(1, SC_GATHER_ROWS // blk)
    nb = -(-nb // group) * group
    counts = counts.astype(jnp.int32)
    padded = ((counts + blk - 1) // blk) * blk
    pends = jnp.cumsum(padded)
    pstarts = pends - padded
    block_start = jnp.arange(nb, dtype=jnp.int32) * blk
    blk_expert = jnp.minimum(jnp.sum(block_start[:, None] >= pends[None, :], axis=1),
                             N_EXPERTS - 1).astype(jnp.int32)
    nwin = padded // SC_GATHER_WINDOW
    def offset(r, e):
        return (r % SC_GATHER_WINDOW) * nwin[e] + r // SC_GATHER_WINDOW
    dest = pstarts[idx] + offset(rank, idx)
    tok = jnp.repeat(jnp.arange(n_tokens, dtype=jnp.int32), TOP_K)
    n_pad = nb * blk - A
    pad_per = padded - counts
    padcum = jnp.cumsum(pad_per)
    q = jnp.arange(n_pad, dtype=jnp.int32)
    e_q = jnp.sum(q[:, None] >= padcum[None, :], axis=1).astype(jnp.int32)
    e_c = jnp.minimum(e_q, N_EXPERTS - 1)
    key_in = pstarts[e_c] + offset(counts[e_c] + q - (padcum - pad_per)[e_c], e_c)
    key_tail = pends[-1] - padcum[-1] + q
    pad_keys = jnp.where(e_q < N_EXPERTS, key_in, key_tail)
    keys = jnp.concatenate([dest.reshape(A), pad_keys])
    vals = jnp.concatenate([tok, pad_keys % n_tokens])
    _, tok_of_slot = lax.sort((keys, vals), num_keys=1, is_stable=False)
    return dest, tok_of_slot, blk_expert


def kernel(x, c, positions, mod_w, mod_b, norm1_g, w_in, conv_a_w, w_a_out, ret_norm_g, w_b_out, glu_b, conv_c_w, conv_c_b, ln_c_g, ln_c_b, w_c_out, w_o, norm2_g, router_w, router_b, exp_w1, exp_b1, exp_w2, exp_b2, final_g):
    B, S, D = x.shape
    L = mod_w.shape[0]
    N = B * S
    da = w_a_out.shape[1]
    dr = w_b_out.shape[1]
    dc = w_c_out.shape[1]
    s1 = 3 * da
    s2 = s1 + 4 * dr
    s3 = s2 + 2 * dc

    mod = _modulation(c, mod_w, mod_b)
    cosf, sinf = _rope_tables(positions)
    tables = _retention_tables()
    row = lambda v: v.reshape(1, -1)
    rw_pad = jnp.zeros((L, D, LANES), F32).at[:, :, :N_EXPERTS].set(router_w)
    rb_pad = jnp.zeros((L, 1, LANES), F32).at[:, 0, :N_EXPERTS].set(router_b)

    for l in range(L):
        sh1, sc1, g1, sh2, sc2, g2 = [mod[l, :, i * D:(i + 1) * D].reshape(B, 1, D) for i in range(6)]
        wl = w_in[l].astype(BF16)
        wg = wl[:, s3:]
        n1 = row(norm1_g[l])
        ma = _branch_a(x, n1, sh1, sc1, wl[:, :s1], wg[:, :D], conv_a_w[l],
                       w_a_out[l].astype(BF16))
        mb = _branch_b(x, n1, sh1, sc1, cosf, sinf, wl[:, s1:s2], wg[:, D:2 * D], tables,
                       row(ret_norm_g[l]), w_b_out[l].astype(BF16))
        mc = _branch_c(x, n1, sh1, sc1, wl[:, s2:s3], wg[:, 2 * D:], row(glu_b[l]), conv_c_w[l],
                       row(conv_c_b[l]), row(ln_c_g[l]), row(ln_c_b[l]), w_c_out[l].astype(BF16))
        x1, idx, prob, rank, counts, *h2_parts = _out_router(
            ma, mb, mc, x, g1, w_o[l].astype(BF16), row(norm2_g[l]), sh2, sc2, rw_pad[l], rb_pad[l])
        idx = idx.reshape(N, LANES)[:, :TOP_K]
        rank = rank.reshape(N, LANES)[:, :TOP_K]
        dest, tok_of_slot, blk_expert = _routing_tables(counts[0, :N_EXPERTS], idx, rank, N)
        xparts = [_sc_row_gather(p, tok_of_slot) for p in h2_parts]
        yparts = _experts(xparts, blk_expert, exp_w1, exp_b1, exp_w2, exp_b2, layer=l)
        tmc = min(COMBINE_TILE, S)
        dest_kmajor = dest.reshape(N // tmc, tmc, TOP_K).transpose(0, 2, 1).reshape(N * TOP_K)
        ycat = [_sc_row_gather(p, dest_kmajor) for p in yparts]
        x = _combine(ycat, x1.reshape(N, D), g2, prob.reshape(N, LANES), row(final_g),
                     S, final_norm=(l == L - 1)).reshape(B, S, D)
    return x
```

```python
import functools

import jax
import jax.numpy as jnp
from jax import lax
from jax.experimental import pallas as pl
from jax.experimental.pallas import tpu as pltpu
from jax.experimental.pallas import tpu_sc as plsc

F32 = jnp.float32
BF16 = jnp.bfloat16

EPS = 1e-6
SHORT_CONV_W = 3
RET_HEADS = 8
RET_HEAD_DIM = 128
RET_CHUNK = 128
ROPE_BASE = 10000.0
CONF_KERNEL = 31
N_EXPERTS = 32
TOP_K = 4
SWIGLU_LIMIT = 7.0
SWIGLU_ALPHA = 1.702

LANES = 128
SUBLANES = 8
VMEM_LIMIT = 56 * 1024 * 1024

ROW_TILE = 512
EXPERT_BLOCK = 512
SC_GATHER_WINDOW = 128
MOE_SPLIT = 2
SC_SUBCORES = 32
SC_GATHER_ROWS = SC_GATHER_WINDOW * SC_SUBCORES
COMBINE_TILE = 512
CONV_ROWS = 32
CONF_HALO = 32


def _params(n_grid):
    return pltpu.CompilerParams(
        dimension_semantics=("arbitrary",) * n_grid, vmem_limit_bytes=VMEM_LIMIT)


def _const_spec(shape):
    nd = len(shape)
    return pl.BlockSpec(shape, lambda *_: (0,) * nd, pipeline_mode=pl.Buffered(1))


def _sigmoid(x):
    return 1.0 / (1.0 + jnp.exp(-x))


def _silu(x):
    return x * _sigmoid(x)


def _norm_mod(x, g, sh, sc):
    y = x * lax.rsqrt(jnp.mean(x * x, axis=-1, keepdims=True) + EPS)
    return (y * g) * (1.0 + sc) + sh


def _bdot(a, b):
    return jnp.dot(a.astype(BF16), b.astype(BF16), preferred_element_type=F32)


def _pack_rows(v):
    c = v.shape[1] // MOE_SPLIT
    parts = []
    for p in range(MOE_SPLIT):
        lo = v[:, p * c:p * c + c // 2].astype(BF16).astype(F32)
        hi = v[:, p * c + c // 2:(p + 1) * c].astype(BF16).astype(F32)
        lo_bits = lax.bitcast_convert_type(lo, jnp.uint32) >> 16
        hi_bits = lax.bitcast_convert_type(hi, jnp.uint32) & jnp.uint32(0xFFFF0000)
        parts.append(hi_bits | lo_bits)
    return parts


def _unpack_rows(parts):
    cols = []
    for w in parts:
        cols.append(lax.bitcast_convert_type(w << 16, F32))
        cols.append(lax.bitcast_convert_type(w & jnp.uint32(0xFFFF0000), F32))
    return jnp.concatenate(cols, axis=1)


def _mod_kernel(c_ref, w_ref, b_ref, o_ref):
    cond = _silu(c_ref[...])
    o_ref[0] = jnp.dot(cond, w_ref[0], preferred_element_type=F32,
                       precision=lax.Precision.HIGHEST) + b_ref[0]


def _modulation(c, mod_w, mod_b):
    L, D, M = mod_w.shape
    B = c.shape[0]
    tn = 1024
    return pl.pallas_call(
        _mod_kernel,
        grid=(L, M // tn),
        in_specs=[pl.BlockSpec((B, D), lambda l, j: (0, 0)),
                  pl.BlockSpec((1, D, tn), lambda l, j: (l, 0, j)),
                  pl.BlockSpec((1, 1, tn), lambda l, j: (l, 0, j))],
        out_specs=pl.BlockSpec((1, B, tn), lambda l, j: (l, 0, j)),
        out_shape=jax.ShapeDtypeStruct((L, B, M), F32),
        compiler_params=_params(2),
        name="modulation",
    )(c, mod_w, mod_b.reshape(L, 1, M))


def _rope_kernel(pos_ref, invf_ref, cos_ref, sin_ref):
    ang = pos_ref[0].astype(F32) * invf_ref[...]
    lane = lax.broadcasted_iota(jnp.int32, ang.shape, 1)
    s = jnp.sin(ang)
    cos_ref[0] = jnp.cos(ang)
    sin_ref[0] = jnp.where(lane < RET_HEAD_DIM // 2, -s, s)


def _rope_tables(positions):
    B, S = positions.shape
    inv_freq = 1.0 / (ROPE_BASE ** (jnp.arange(0, RET_HEAD_DIM, 2, dtype=F32) / RET_HEAD_DIM))
    invf = jnp.concatenate([inv_freq, inv_freq]).reshape(1, RET_HEAD_DIM)
    ts = min(S, 1024)
    spec = pl.BlockSpec((1, ts, RET_HEAD_DIM), lambda b, j: (b, j, 0))
    return pl.pallas_call(
        _rope_kernel,
        grid=(B, S // ts),
        in_specs=[pl.BlockSpec((1, ts, 1), lambda b, j: (b, j, 0)),
                  pl.BlockSpec((1, RET_HEAD_DIM), lambda b, j: (0, 0))],
        out_specs=[spec, spec],
        out_shape=[jax.ShapeDtypeStruct((B, S, RET_HEAD_DIM), F32)] * 2,
        compiler_params=_params(2),
        name="rope_tables",
    )(positions.reshape(B, S, 1), invf)


def _branch_a_kernel(x_ref, g_ref, sh_ref, sc_ref, w_ref, wg_ref, cw_ref, wo_ref, o_ref, buf_ref):
    tm = x_ref.shape[1]
    da = wo_ref.shape[0]

    @pl.when(pl.program_id(1) == 0)
    def _():
        buf_ref[0:SUBLANES, :] = jnp.zeros((SUBLANES, da), F32)

    h = _norm_mod(x_ref[0], g_ref[...], sh_ref[0], sc_ref[0]).astype(BF16)
    p = jnp.dot(h, w_ref[...], preferred_element_type=F32)
    b_g, c_g, xa = p[:, :da], p[:, da:2 * da], p[:, 2 * da:]
    u = c_g * xa
    buf_ref[SUBLANES:SUBLANES + tm, :] = u
    conv = cw_ref[2:3, :] * u
    for j in range(SHORT_CONV_W - 1):
        off = SUBLANES - (SHORT_CONV_W - 1) + j
        conv = conv + cw_ref[j:j + 1, :] * buf_ref[off:off + tm, :]
    buf_ref[0:SUBLANES, :] = buf_ref[tm:tm + SUBLANES, :]
    y_a = _bdot(b_g * conv, wo_ref[...])
    gate = _sigmoid(jnp.dot(h, wg_ref[...], preferred_element_type=F32))
    o_ref[0] = (gate * y_a).astype(BF16)


def _branch_a(x, g, sh, sc, w, wg, cw, wo):
    B, S, D = x.shape
    da = wo.shape[0]
    tm = min(ROW_TILE, S)
    vec = pl.BlockSpec((1, 1, D), lambda b, j: (b, 0, 0))
    return pl.pallas_call(
        _branch_a_kernel,
        grid=(B, S // tm),
        in_specs=[pl.BlockSpec((1, tm, D), lambda b, j: (b, j, 0)),
                  _const_spec((1, D)), vec, vec,
                  _const_spec(w.shape), _const_spec(wg.shape), _const_spec(cw.shape),
                  _const_spec(wo.shape)],
        out_specs=pl.BlockSpec((1, tm, D), lambda b, j: (b, j, 0)),
        out_shape=jax.ShapeDtypeStruct((B, S, D), BF16),
        scratch_shapes=[pltpu.VMEM((tm + SUBLANES, da), F32)],
        compiler_params=_params(2),
        name="branch_a",
    )(x, g, sh, sc, w, wg, cw, wo)


def _branch_c_kernel(x_ref, g_ref, sh_ref, sc_ref, w_ref, wg_ref, glub_ref, cw_ref, cb_ref,
                     lng_ref, lnb_ref, wo_ref, o_ref, buf_ref, conv_ref):
    tm = x_ref.shape[1]
    dc = wo_ref.shape[0]

    @pl.when(pl.program_id(1) == 0)
    def _():
        buf_ref[0, 0:CONF_HALO, :] = jnp.zeros((CONF_HALO, dc), F32)

    h = _norm_mod(x_ref[0], g_ref[...], sh_ref[0], sc_ref[0]).astype(BF16)
    p = jnp.dot(h, w_ref[...], preferred_element_type=F32) + glub_ref[...]
    buf_ref[0, CONF_HALO:CONF_HALO + tm, :] = p[:, :dc] * _sigmoid(p[:, dc:])
    for b in range(1, SUBLANES):
        buf_ref[b, SUBLANES:, :] = buf_ref[0, SUBLANES - b:tm + CONF_HALO - b, :]

    base = CONF_HALO - (CONF_KERNEL - 1)

    for s in range(tm // CONV_ROWS):
        r0 = s * CONV_ROWS
        acc = jnp.zeros((CONV_ROWS, dc), F32)
        for j in range(CONF_KERNEL):
            a = -(-(base + j) // SUBLANES)
            b = a * SUBLANES - (base + j)
            acc = acc + cw_ref[j:j + 1, :] * buf_ref[b, pl.ds(r0 + a * SUBLANES, CONV_ROWS), :]
        conv_ref[pl.ds(r0, CONV_ROWS), :] = acc
    buf_ref[0, 0:CONF_HALO, :] = buf_ref[0, tm:tm + CONF_HALO, :]

    u = conv_ref[...] + cb_ref[...]
    mu = jnp.mean(u, axis=-1, keepdims=True)
    d = u - mu
    var = jnp.mean(d * d, axis=-1, keepdims=True)
    y = d * lax.rsqrt(var + EPS) * lng_ref[...] + lnb_ref[...]
    y_c = _bdot(_silu(y), wo_ref[...])
    gate = _sigmoid(jnp.dot(h, wg_ref[...], preferred_element_type=F32))
    o_ref[0] = (gate * y_c).astype(BF16)


def _branch_c(x, g, sh, sc, w, wg, glub, cw, cb, lng, lnb, wo):
    B, S, D = x.shape
    dc = wo.shape[0]
    tm = min(ROW_TILE, S)
    vec = pl.BlockSpec((1, 1, D), lambda b, j: (b, 0, 0))
    return pl.pallas_call(
        _branch_c_kernel,
        grid=(B, S // tm),
        in_specs=[pl.BlockSpec((1, tm, D), lambda b, j: (b, j, 0)),
                  _const_spec((1, D)), vec, vec,
                  _const_spec(w.shape), _const_spec(wg.shape), _const_spec(glub.shape),
                  _const_spec(cw.shape), _const_spec(cb.shape), _const_spec(lng.shape),
                  _const_spec(lnb.shape), _const_spec(wo.shape)],
        out_specs=pl.BlockSpec((1, tm, D), lambda b, j: (b, j, 0)),
        out_shape=jax.ShapeDtypeStruct((B, S, D), BF16),
        scratch_shapes=[pltpu.VMEM((SUBLANES, tm + CONF_HALO, dc), F32), pltpu.VMEM((tm, dc), F32)],
        compiler_params=_params(2),
        name="branch_c",
    )(x, g, sh, sc, w, wg, glub, cw, cb, lng, lnb, wo)


def _branch_b_kernel(x_ref, g_ref, sh_ref, sc_ref, cos_ref, sin_ref, w_ref, wg_ref,
                     dmask_ref, xi_ref, zeta_ref, gch_ref, rg_ref, wo_ref, o_ref,
                     q_ref, k_ref, v_ref, r_ref, state_ref):
    tm = x_ref.shape[1]
    dr = wo_ref.shape[0]
    hd = RET_HEAD_DIM

    @pl.when(pl.program_id(1) == 0)
    def _():
        state_ref[...] = jnp.zeros(state_ref.shape, F32)

    h = _norm_mod(x_ref[0], g_ref[...], sh_ref[0], sc_ref[0]).astype(BF16)
    cos = cos_ref[0]
    sin = sin_ref[0]
    scale = hd ** -0.5

    def rope(t):
        return t * cos + pltpu.roll(t, hd // 2, axis=1) * sin

    q = jnp.dot(h, w_ref[:, 0:dr], preferred_element_type=F32) * scale
    k = jnp.dot(h, w_ref[:, dr:2 * dr], preferred_element_type=F32)
    for hh in range(RET_HEADS):
        sl = slice(hh * hd, (hh + 1) * hd)
        q_ref[:, sl] = rope(q[:, sl]).astype(BF16)
        k_ref[:, sl] = rope(k[:, sl])
    v_ref[...] = jnp.dot(h, w_ref[:, 2 * dr:3 * dr], preferred_element_type=F32).astype(BF16)

    for ci in range(tm // RET_CHUNK):
        rows = pl.ds(ci * RET_CHUNK, RET_CHUNK)
        for hh in range(RET_HEADS):
            sl = slice(hh * hd, (hh + 1) * hd)
            qh = q_ref[rows, sl]
            kh = k_ref[rows, sl]
            vh = v_ref[rows, sl]
            st = state_ref[hh]
            scores = lax.dot_general(qh, kh.astype(BF16), (((1,), (1,)), ((), ())),
                                     preferred_element_type=F32) * dmask_ref[hh]
            inner = _bdot(scores, vh)
            cross = _bdot(qh, st) * xi_ref[hh]
            kz = (kh * zeta_ref[hh]).astype(BF16)
            state_ref[hh] = st * gch_ref[hh] + lax.dot_general(
                kz, vh, (((0,), (0,)), ((), ())), preferred_element_type=F32)
            o = inner + cross
            mu = jnp.mean(o, axis=-1, keepdims=True)
            d = o - mu
            var = jnp.mean(d * d, axis=-1, keepdims=True)
            r_ref[rows, sl] = d * lax.rsqrt(var + EPS)

    og = jnp.dot(h, w_ref[:, 3 * dr:4 * dr], preferred_element_type=F32)
    y_b = _bdot(_silu(og) * (r_ref[...] * rg_ref[...]), wo_ref[...])
    gate = _sigmoid(jnp.dot(h, wg_ref[...], preferred_element_type=F32))
    o_ref[0] = (gate * y_b).astype(BF16)


def _retention_tables():
    H, C = RET_HEADS, RET_CHUNK
    gamma = 1.0 - jnp.power(2.0, -5.0 - jnp.arange(H, dtype=F32))
    lg = jnp.log(gamma)
    idx = jnp.arange(C, dtype=F32)
    rel = idx[:, None] - idx[None, :]
    dmask = jnp.where(rel[None] >= 0, jnp.exp(jnp.maximum(rel, 0.0)[None] * lg[:, None, None]), 0.0)
    xi = jnp.exp((idx + 1.0)[None] * lg[:, None])
    zeta = jnp.exp((C - 1.0 - idx)[None] * lg[:, None])
    g_chunk = jnp.exp(C * lg)
    bc = lambda t: jnp.broadcast_to(t[:, :, None], (H, C, RET_HEAD_DIM)).astype(F32)
    g_rows = jnp.broadcast_to(g_chunk[:, None, None], (H, 1, RET_HEAD_DIM)).astype(F32)
    return dmask.astype(F32), bc(xi), bc(zeta), g_rows


def _branch_b(x, g, sh, sc, cosf, sinf, w, wg, tables, rg, wo):
    B, S, D = x.shape
    dr = wo.shape[0]
    tm = min(ROW_TILE, S)
    dmask, xi, zeta, g_chunk = tables
    vec = pl.BlockSpec((1, 1, D), lambda b, j: (b, 0, 0))
    rope_spec = pl.BlockSpec((1, tm, RET_HEAD_DIM), lambda b, j: (b, j, 0))
    return pl.pallas_call(
        _branch_b_kernel,
        grid=(B, S // tm),
        in_specs=[pl.BlockSpec((1, tm, D), lambda b, j: (b, j, 0)),
                  _const_spec((1, D)), vec, vec, rope_spec, rope_spec,
                  _const_spec(w.shape), _const_spec(wg.shape), _const_spec(dmask.shape),
                  _const_spec(xi.shape), _const_spec(zeta.shape), _const_spec(g_chunk.shape),
                  _const_spec(rg.shape), _const_spec(wo.shape)],
        out_specs=pl.BlockSpec((1, tm, D), lambda b, j: (b, j, 0)),
        out_shape=jax.ShapeDtypeStruct((B, S, D), BF16),
        scratch_shapes=[pltpu.VMEM((tm, dr), BF16), pltpu.VMEM((tm, dr), F32),
                        pltpu.VMEM((tm, dr), BF16), pltpu.VMEM((tm, dr), F32),
                        pltpu.VMEM((RET_HEADS, RET_HEAD_DIM, RET_HEAD_DIM), F32)],
        compiler_params=_params(2),
        name="branch_b",
    )(x, g, sh, sc, cosf, sinf, w, wg, dmask, xi, zeta, g_chunk, rg, wo)


def _out_router_kernel(ma_ref, mb_ref, mc_ref, x_ref, g1_ref, wo_ref, n2_ref, sh_ref, sc_ref,
                       rw_ref, rb_ref, x1_ref, idx_ref, prob_ref, rank_ref, cnt_ref, *rest):
    h2_refs, carry_ref = rest[:MOE_SPLIT], rest[MOE_SPLIT]
    tm = x_ref.shape[1]
    first = jnp.logical_and(pl.program_id(0) == 0, pl.program_id(1) == 0)

    @pl.when(first)
    def _():
        carry_ref[...] = jnp.zeros(carry_ref.shape, F32)

    m = ma_ref[0].astype(F32) + mb_ref[0].astype(F32) + mc_ref[0].astype(F32)
    x1 = x_ref[0] + g1_ref[0] * _bdot(m, wo_ref[...])
    x1_ref[0] = x1
    h2 = _norm_mod(x1, n2_ref[...], sh_ref[0], sc_ref[0])
    for h2_ref, part in zip(h2_refs, _pack_rows(h2)):
        h2_ref[...] = part

    rw = rw_ref[...]
    h_hi = h2.astype(BF16)
    h_lo = (h2 - h_hi.astype(F32)).astype(BF16)
    rw_hi = rw.astype(BF16)
    rw_lo = (rw - rw_hi.astype(F32)).astype(BF16)
    logits = (jnp.dot(h_hi, rw_hi, preferred_element_type=F32)
              + jnp.dot(h_hi, rw_lo, preferred_element_type=F32)
              + jnp.dot(h_lo, rw_hi, preferred_element_type=F32)) + rb_ref[...]
    lane = lax.broadcasted_iota(jnp.int32, logits.shape, 1)
    lane_f = lane.astype(F32)
    neg = jnp.float32(-jnp.inf)
    work = jnp.where(lane < N_EXPERTS, logits, neg)
    idx_out = jnp.zeros(logits.shape, jnp.int32)
    val_out = jnp.full(logits.shape, neg, F32)
    onehot_all = jnp.zeros(logits.shape, F32)
    sels = []
    for kk in range(TOP_K):
        mx = jnp.max(work, axis=-1, keepdims=True)
        first_idx = jnp.min(jnp.where(work == mx, lane_f, float(LANES)), axis=-1, keepdims=True)
        sel = lane_f == first_idx
        sels.append(sel)
        idx_out = jnp.where(lane == kk, first_idx.astype(jnp.int32), idx_out)
        val_out = jnp.where(lane == kk, mx, val_out)
        onehot_all = onehot_all + sel.astype(F32)
        work = jnp.where(sel, neg, work)

    ex = jnp.exp(val_out - val_out[:, 0:1])
    prob_ref[0] = ex / jnp.sum(ex, axis=-1, keepdims=True)
    idx_ref[0] = idx_out

    row = lax.broadcasted_iota(jnp.int32, (tm, tm), 0)
    col = lax.broadcasted_iota(jnp.int32, (tm, tm), 1)
    lower = (col < row).astype(BF16)
    before = jnp.dot(lower, onehot_all.astype(BF16), preferred_element_type=F32) + carry_ref[...]
    rank_out = jnp.zeros(logits.shape, jnp.int32)
    for kk in range(TOP_K):
        rk = jnp.sum(jnp.where(sels[kk], before, 0.0), axis=-1, keepdims=True)
        rank_out = jnp.where(lane == kk, rk.astype(jnp.int32), rank_out)
    rank_ref[0] = rank_out
    total = carry_ref[...] + jnp.sum(onehot_all, axis=0, keepdims=True)
    carry_ref[...] = total
    cnt_ref[...] = total


def _out_router(ma, mb, mc, x, g1, wo, n2, sh2, sc2, rw, rb):
    B, S, D = x.shape
    tm = min(ROW_TILE, S)
    tile = pl.BlockSpec((1, tm, D), lambda b, j: (b, j, 0))
    vec = pl.BlockSpec((1, 1, D), lambda b, j: (b, 0, 0))
    lanes = pl.BlockSpec((1, tm, LANES), lambda b, j: (b, j, 0))
    dq = D // (2 * MOE_SPLIT)
    per_seq = S // tm
    part = pl.BlockSpec((tm, dq), lambda b, j: (b * per_seq + j, 0))
    return pl.pallas_call(
        _out_router_kernel,
        grid=(B, S // tm),
        in_specs=[tile, tile, tile, tile, vec, _const_spec(wo.shape), _const_spec((1, D)),
                  vec, vec, _const_spec(rw.shape), _const_spec(rb.shape)],
        out_specs=[tile, lanes, lanes, lanes, pl.BlockSpec((1, LANES), lambda b, j: (0, 0))]
        + [part] * MOE_SPLIT,
        out_shape=[jax.ShapeDtypeStruct((B, S, D), F32),
                   jax.ShapeDtypeStruct((B, S, LANES), jnp.int32),
                   jax.ShapeDtypeStruct((B, S, LANES), F32),
                   jax.ShapeDtypeStruct((B, S, LANES), jnp.int32),
                   jax.ShapeDtypeStruct((1, LANES), F32)]
        + [jax.ShapeDtypeStruct((B * S, dq), jnp.uint32)] * MOE_SPLIT,
        scratch_shapes=[pltpu.VMEM((1, LANES), F32)],
        compiler_params=_params(2),
        name="out_router",
    )(ma, mb, mc, x, g1, wo, n2, sh2, sc2, rw, rb)


def _sc_row_gather(src, indices):
    n = indices.shape[0]
    d = src.shape[1]
    w = SC_GATHER_WINDOW
    mesh = plsc.VectorSubcoreMesh(core_axis_name="core", subcore_axis_name="subcore")
    per = n // w // SC_SUBCORES
    window = lambda i: (i % per) * SC_SUBCORES + i // per

    @pl.kernel(out_type=jax.ShapeDtypeStruct((n, d), src.dtype), mesh=mesh, scratch_types=[])
    def gather_kernel(src_hbm, idx_hbm, out_hbm):
        def body(idx_vmem, out_vmem):
            pltpu.sync_copy(src_hbm.at[idx_vmem.at[0]], out_vmem)

        pltpu.emit_pipeline(
            body,
            grid=(n // w,),
            in_specs=[pl.BlockSpec((1, w), index_map=lambda i: (0, window(i)))],
            out_specs=[pl.BlockSpec((w, d), index_map=lambda i: (window(i), 0))],
            core_axis_name=("core", "subcore"),
            dimension_semantics=(pltpu.PARALLEL,),
        )(idx_hbm, out_hbm)

    return gather_kernel(src, indices.reshape(1, n))


def _expert_kernel(be_ref, *refs):
    x_refs, refs = refs[:MOE_SPLIT], refs[MOE_SPLIT:]
    w1_ref, b1_ref, w2_ref, b2_ref = refs[:4]
    o_refs, (w1b_ref, w2b_ref) = refs[4:4 + MOE_SPLIT], refs[4 + MOE_SPLIT:]
    i = pl.program_id(0)
    f = w2_ref.shape[2]

    @pl.when(jnp.logical_or(i == 0, be_ref[i] != be_ref[jnp.maximum(i - 1, 0)]))
    def _():
        w1b_ref[...] = w1_ref[0, 0].astype(BF16)
        w2b_ref[...] = w2_ref[0, 0].astype(BF16)

    xb = _unpack_rows([x_ref[...] for x_ref in x_refs]).astype(BF16)
    gu = jnp.dot(xb, w1b_ref[...], preferred_element_type=F32) + b1_ref[0, 0]
    glu = jnp.minimum(gu[:, :f], SWIGLU_LIMIT)
    lin = jnp.clip(gu[:, f:], -SWIGLU_LIMIT, SWIGLU_LIMIT)
    act = glu * _sigmoid(SWIGLU_ALPHA * glu) * (lin + 1.0)
    y = _bdot(act, w2b_ref[...]) + b2_ref[0, 0]
    for o_ref, part in zip(o_refs, _pack_rows(y)):
        o_ref[...] = part


def _experts(xparts, blk_expert, w1, b1, w2, b2, layer):
    rows, dq = xparts[0].shape
    L, E, D, F2 = w1.shape
    nb = blk_expert.shape[0]
    blk = EXPERT_BLOCK
    per_expert = lambda shape: pl.BlockSpec((1, 1) + shape, lambda i, be: (layer, be[i], 0, 0))
    grid_spec = pltpu.PrefetchScalarGridSpec(
        num_scalar_prefetch=1,
        grid=(nb,),
        in_specs=[pl.BlockSpec((blk, dq), lambda i, be: (i, 0))] * MOE_SPLIT
        + [per_expert((D, F2)), per_expert((1, F2)), per_expert((F2 // 2, D)), per_expert((1, D))],
        out_specs=[pl.BlockSpec((blk, dq), lambda i, be: (i, 0))] * MOE_SPLIT,
        scratch_shapes=[pltpu.VMEM((D, F2), BF16), pltpu.VMEM((F2 // 2, D), BF16)])
    return pl.pallas_call(
        _expert_kernel,
        grid_spec=grid_spec,
        out_shape=[jax.ShapeDtypeStruct((rows, dq), jnp.uint32)] * MOE_SPLIT,
        compiler_params=_params(1),
        name="experts",
    )(blk_expert, *xparts, w1, b1.reshape(L, E, 1, F2), w2, b2.reshape(L, E, 1, D))


def _combine_kernel(*refs, final_norm):
    y_refs, (x1_ref, g2_ref, prob_ref, fg_ref, o_ref) = refs[:MOE_SPLIT], refs[MOE_SPLIT:]
    tm = x1_ref.shape[0]
    prob = prob_ref[...]
    acc = jnp.zeros(x1_ref.shape, F32)
    for kk in range(TOP_K):
        rows = slice(kk * tm, (kk + 1) * tm)
        acc = acc + prob[:, kk:kk + 1] * _unpack_rows([y_ref[rows, :] for y_ref in y_refs])
    x2 = x1_ref[...] + g2_ref[0] * acc
    if final_norm:
        x2 = x2 * lax.rsqrt(jnp.mean(x2 * x2, axis=-1, keepdims=True) + EPS) * fg_ref[...]
    o_ref[...] = x2


def _combine(yparts, x1, g2, prob, final_g, seq_len, final_norm):
    N, D = x1.shape
    tm = min(COMBINE_TILE, seq_len)
    per_seq = seq_len // tm
    return pl.pallas_call(
        functools.partial(_combine_kernel, final_norm=final_norm),
        grid=(N // tm,),
        in_specs=[pl.BlockSpec((TOP_K * tm, D // (2 * MOE_SPLIT)), lambda i: (i, 0))] * MOE_SPLIT
        + [pl.BlockSpec((tm, D), lambda i: (i, 0)),
                  pl.BlockSpec((1, 1, D), lambda i: (i // per_seq, 0, 0)),
                  pl.BlockSpec((tm, LANES), lambda i: (i, 0)),
                  pl.BlockSpec((1, D), lambda i: (0, 0))],
        out_specs=pl.BlockSpec((tm, D), lambda i: (i, 0)),
        out_shape=jax.ShapeDtypeStruct((N, D), F32),
        compiler_params=_params(1),
        name="combine",
    )(*yparts, x1, g2, prob, final_g)


def _routing_tables(counts, idx, rank, n_tokens):
    blk = EXPERT_BLOCK
    A = n_tokens * TOP_K
    nb = (A + N_EXPERTS * (blk - 1)) // blk
    group = max(1, SC_GATHER_ROWS // blk)
    nb = -(-nb // group) * group
    counts = counts.astype(jnp.int32)
    padded = ((counts + blk - 1) // blk) * blk
    pends = jnp.cumsum(padded)
    pstarts = pends - padded
    block_start = jnp.arange(nb, dtype=jnp.int32) * blk
    blk_expert = jnp.minimum(jnp.sum(block_start[:, None] >= pends[None, :], axis=1),
                             N_EXPERTS - 1).astype(jnp.int32)
    nwin = padded // SC_GATHER_WINDOW
    def offset(r, e):
        return (r % SC_GATHER_WINDOW) * nwin[e] + r // SC_GATHER_WINDOW
    dest = pstarts[idx] + offset(rank, idx)
    tok = jnp.repeat(jnp.arange(n_tokens, dtype=jnp.int32), TOP_K)
    n_pad = nb * blk - A
    pad_per = padded - counts
    padcum = jnp.cumsum(pad_per)
    q = jnp.arange(n_pad, dtype=jnp.int32)
    e_q = jnp.sum(q[:, None] >= padcum[None, :], axis=1).astype(jnp.int32)
    e_c = jnp.minimum(e_q, N_EXPERTS - 1)
    key_in = pstarts[e_c] + offset(counts[e_c] + q - (padcum - pad_per)[e_c], e_c)
    key_tail = pends[-1] - padcum[-1] + q
    pad_keys = jnp.where(e_q < N_EXPERTS, key_in, key_tail)
    keys = jnp.concatenate([dest.reshape(A), pad_keys])
    vals = jnp.concatenate([tok, pad_keys % n_tokens])
    _, tok_of_slot = lax.sort((keys, vals), num_keys=1, is_stable=False)
    return dest, tok_of_slot, blk_expert


def kernel(x, c, positions, mod_w, mod_b, norm1_g, w_in, conv_a_w, w_a_out, ret_norm_g, w_b_out, glu_b, conv_c_w, conv_c_b, ln_c_g, ln_c_b, w_c_out, w_o, norm2_g, router_w, router_b, exp_w1, exp_b1, exp_w2, exp_b2, final_g):
    B, S, D = x.shape
    L = mod_w.shape[0]
    N = B * S
    da = w_a_out.shape[1]
    dr = w_b_out.shape[1]
    dc = w_c_out.shape[1]
    s1 = 3 * da
    s2 = s1 + 4 * dr
    s3 = s2 + 2 * dc

    mod = _modulation(c, mod_w, mod_b)
    cosf, sinf = _rope_tables(positions)
    tables = _retention_tables()
    row = lambda v: v.reshape(1, -1)
    rw_pad = jnp.zeros((L, D, LANES), F32).at[:, :, :N_EXPERTS].set(router_w)
    rb_pad = jnp.zeros((L, 1, LANES), F32).at[:, 0, :N_EXPERTS].set(router_b)

    for l in range(L):
        sh1, sc1, g1, sh2, sc2, g2 = [mod[l, :, i * D:(i + 1) * D].reshape(B, 1, D) for i in range(6)]
        wl = w_in[l].astype(BF16)
        wg = wl[:, s3:]
        n1 = row(norm1_g[l])
        ma = _branch_a(x, n1, sh1, sc1, wl[:, :s1], wg[:, :D], conv_a_w[l],
                       w_a_out[l].astype(BF16))
        mb = _branch_b(x, n1, sh1, sc1, cosf, sinf, wl[:, s1:s2], wg[:, D:2 * D], tables,
                       row(ret_norm_g[l]), w_b_out[l].astype(BF16))
        mc = _branch_c(x, n1, sh1, sc1, wl[:, s2:s3], wg[:, 2 * D:], row(glu_b[l]), conv_c_w[l],
                       row(conv_c_b[l]), row(ln_c_g[l]), row(ln_c_b[l]), w_c_out[l].astype(BF16))
        x1, idx, prob, rank, counts, *h2_parts = _out_router(
            ma, mb, mc, x, g1, w_o[l].astype(BF16), row(norm2_g[l]), sh2, sc2, rw_pad[l], rb_pad[l])
        idx = idx.reshape(N, LANES)[:, :TOP_K]
        rank = rank.reshape(N, LANES)[:, :TOP_K]
        dest, tok_of_slot, blk_expert = _routing_tables(counts[0, :N_EXPERTS], idx, rank, N)
        xparts = [_sc_row_gather(p, tok_of_slot) for p in h2_parts]
        yparts = _experts(xparts, blk_expert, exp_w1, exp_b1, exp_w2, exp_b2, layer=l)
        tmc = min(COMBINE_TILE, S)
        dest_kmajor = dest.reshape(N // tmc, tmc, TOP_K).transpose(0, 2, 1).reshape(N * TOP_K)
        ycat = [_sc_row_gather(p, dest_kmajor) for p in yparts]
        x = _combine(ycat, x1.reshape(N, D), g2, prob.reshape(N, LANES), row(final_g),
                     S, final_norm=(l == L - 1)).reshape(B, S, D)
    return x
```
